```python
import jax, jax.numpy as jnp
from jax import lax
import numpy as np

D_MODEL = 1024
BATCH = 4
SEQ = 8192
DEPTH = 1

D_MIX = D_MODEL
D_LRU = D_MIX // 2
LRU_HEADS = 8
LRU_HEAD_DIM = D_LRU // LRU_HEADS
LRU_CONV = 4
LRU_C = 8.0
D_RET = D_MIX - D_LRU
RET_HEADS = 4
RET_HEAD_DIM = D_RET // RET_HEADS
RET_CHUNK = 128
ROPE_BASE = 10000.0
D_IN = 2 * D_LRU + 4 * D_RET
SPLITS = (D_LRU, 2 * D_LRU, 2 * D_LRU + D_RET, 2 * D_LRU + 2 * D_RET, 2 * D_LRU + 3 * D_RET)
D_FF = 3 * D_MODEL
FFN_CONV = 3
NORM_EPS = 1e-6

kernel_name = "hymba_rglru_retention_convffn_block"


def rms_norm(x, gain):
    xf = x.astype(jnp.float32)
    y = xf * lax.rsqrt(jnp.mean(xf * xf, axis=-1, keepdims=True) + NORM_EPS)
    return (y * gain.astype(jnp.float32)).astype(x.dtype)


def causal_depthwise_conv(x, w, b):
    width, ch = w.shape
    y = lax.conv_general_dilated(
        x, w[:, None, :].astype(x.dtype), window_strides=(1,),
        padding=[(width - 1, 0)], dimension_numbers=("NWC", "WIO", "NWC"),
        feature_group_count=ch)
    return y + b.astype(x.dtype)


def _linear_recurrence_combine(left, right):
    a1, b1 = left
    a2, b2 = right
    return a1 * a2, a2 * b1 + b2


def rg_lru(x, wa, ba, wx, bx, lam):
    bsz, slen, _ = x.shape
    xf = x.astype(jnp.float32)
    xh = xf.reshape(bsz, slen, LRU_HEADS, LRU_HEAD_DIM)
    r = jax.nn.sigmoid(jnp.einsum("bshi,hij->bshj", xh, wa.astype(jnp.float32)).reshape(bsz, slen, D_LRU)
                       + ba.astype(jnp.float32))
    i = jax.nn.sigmoid(jnp.einsum("bshi,hij->bshj", xh, wx.astype(jnp.float32)).reshape(bsz, slen, D_LRU)
                       + bx.astype(jnp.float32))
    log_a = -LRU_C * r * jax.nn.softplus(-lam.astype(jnp.float32))
    a = jnp.exp(log_a)
    inp = jnp.sqrt(-jnp.expm1(2.0 * log_a)) * (i * xf)
    _, h = lax.associative_scan(_linear_recurrence_combine, (a, inp), axis=1)
    return h


def rotary(x, cos, sin):
    half = x.shape[-1] // 2
    x1, x2 = x[..., :half], x[..., half:]
    return jnp.concatenate([x1 * cos - x2 * sin, x2 * cos + x1 * sin], axis=-1)


def retention_chunkwise(q, k, v):
    bsz, slen, nh, dk = q.shape
    dv = v.shape[-1]
    c = RET_CHUNK
    n = slen // c
    log_g = jnp.log1p(-jnp.exp2(-5.0 - jnp.arange(nh, dtype=jnp.float32)))
    q = q.reshape(bsz, n, c, nh, dk)
    k = (k * (dk ** -0.5)).reshape(bsz, n, c, nh, dk)
    v = v.reshape(bsz, n, c, nh, dv)
    idx = jnp.arange(c, dtype=jnp.float32)
    diff = idx[:, None] - idx[None, :]
    decay_in = jnp.where(diff[None] >= 0, jnp.exp(jnp.maximum(diff, 0.0)[None] * log_g[:, None, None]), 0.0)
    scores = jnp.einsum("bnihd,bnjhd->bnhij", q, k) * decay_in[None, None]
    inner = jnp.einsum("bnhij,bnjhe->bnihe", scores, v)
    zeta = jnp.exp((c - 1 - idx)[None, :] * log_g[:, None])
    kv = jnp.einsum("bnjhd,bnjhe,hj->nbhde", k, v, zeta)
    g_chunk = jnp.exp(c * log_g)[None, :, None, None]

    def step(state, kv_n):
        return state * g_chunk + kv_n, state

    init = jnp.zeros((bsz, nh, dk, dv), jnp.float32)
    _, prev = lax.scan(step, init, kv)
    xi = jnp.exp((idx + 1.0)[None, :] * log_g[:, None])
    cross = jnp.einsum("bnihd,nbhde,hi->bnihe", q, prev, xi)
    return (inner + cross).reshape(bsz, slen, nh, dv)


def head_group_norm(o, gain):
    mu = jnp.mean(o, axis=-1, keepdims=True)
    var = jnp.mean(jnp.square(o - mu), axis=-1, keepdims=True)
    return (o - mu) * lax.rsqrt(var + NORM_EPS) * gain.astype(jnp.float32).reshape(RET_HEADS, RET_HEAD_DIM)


def setup_inputs(seed: int = 0) -> dict:
    key = jax.random.key(seed)
    ks = jax.random.split(key, 20)
    f32 = jnp.float32

    def nrm(k, shape, fan_in):
        return jax.random.normal(k, shape, f32) * (fan_in ** -0.5)

    def gain(k, shape):
        return 1.0 + 0.05 * jax.random.normal(k, shape, f32)

    def bias(k, shape):
        return 0.02 * jax.random.normal(k, shape, f32)

    a0 = jax.random.uniform(ks[9], (DEPTH, D_LRU), f32, minval=0.9, maxval=0.999)
    return {
        "x": jax.random.normal(ks[0], (BATCH, SEQ, D_MODEL), f32),
        "norm1_gain": gain(ks[1], (DEPTH, D_MODEL)),
        "w_in": nrm(ks[2], (DEPTH, D_MODEL, D_IN), D_MODEL),
        "lru_conv_w": nrm(ks[3], (DEPTH, LRU_CONV, D_LRU), LRU_CONV),
        "lru_conv_b": bias(ks[4], (DEPTH, D_LRU)),
        "lru_gate_a_w": nrm(ks[5], (DEPTH, LRU_HEADS, LRU_HEAD_DIM, LRU_HEAD_DIM), LRU_HEAD_DIM),
        "lru_gate_a_b": bias(ks[6], (DEPTH, D_LRU)),
        "lru_gate_x_w": nrm(ks[7], (DEPTH, LRU_HEADS, LRU_HEAD_DIM, LRU_HEAD_DIM), LRU_HEAD_DIM),
        "lru_gate_x_b": bias(ks[8], (DEPTH, D_LRU)),
        "lru_lambda": jnp.log(a0) - jnp.log1p(-a0),
        "lru_norm_gain": gain(ks[10], (DEPTH, D_LRU)),
        "ret_norm_gain": gain(ks[11], (DEPTH, D_RET)),
        "w_out": nrm(ks[12], (DEPTH, D_MIX, D_MODEL), D_MIX),
        "norm2_gain": gain(ks[13], (DEPTH, D_MODEL)),
        "ffn_up_w": nrm(ks[14], (DEPTH, D_MODEL, 2 * D_FF), D_MODEL),
        "ffn_conv_w": nrm(ks[15], (DEPTH, FFN_CONV, 2 * D_FF), FFN_CONV),
        "ffn_conv_b": bias(ks[16], (DEPTH, 2 * D_FF)),
        "ffn_down_w": nrm(ks[17], (DEPTH, D_FF, D_MODEL), D_FF),
        "final_norm_gain": gain(ks[18], (D_MODEL,)),
    }


def reference(x, norm1_gain, w_in, lru_conv_w, lru_conv_b, lru_gate_a_w, lru_gate_a_b,
              lru_gate_x_w, lru_gate_x_b, lru_lambda, lru_norm_gain, ret_norm_gain, w_out,
              norm2_gain, ffn_up_w, ffn_conv_w, ffn_conv_b, ffn_down_w, final_norm_gain):
    bsz, slen, _ = x.shape
    dt = x.dtype
    pos = jnp.arange(slen, dtype=jnp.float32)
    inv_freq = ROPE_BASE ** (-jnp.arange(0, RET_HEAD_DIM, 2, dtype=jnp.float32) / RET_HEAD_DIM)
    ang = pos[:, None] * inv_freq[None, :]
    cos = jnp.cos(ang)[:, None, :]
    sin = jnp.sin(ang)[:, None, :]

    h = x
    for l in range(DEPTH):
        u = rms_norm(h, norm1_gain[l])
        proj = u @ w_in[l].astype(dt)
        x_lru, g_lru, q, k, v, g_ret = jnp.split(proj, SPLITS, axis=-1)

        xc = causal_depthwise_conv(x_lru, lru_conv_w[l], lru_conv_b[l])
        hl = rg_lru(xc, lru_gate_a_w[l], lru_gate_a_b[l], lru_gate_x_w[l], lru_gate_x_b[l], lru_lambda[l])
        y_lru = rms_norm(hl.astype(dt) * jax.nn.gelu(g_lru), lru_norm_gain[l])

        qh = rotary(q.astype(jnp.float32).reshape(bsz, slen, RET_HEADS, RET_HEAD_DIM), cos, sin)
        kh = rotary(k.astype(jnp.float32).reshape(bsz, slen, RET_HEADS, RET_HEAD_DIM), cos, sin)
        vh = v.astype(jnp.float32).reshape(bsz, slen, RET_HEADS, RET_HEAD_DIM)
        o = head_group_norm(retention_chunkwise(qh, kh, vh), ret_norm_gain[l]).reshape(bsz, slen, D_RET)
        y_ret = o.astype(dt) * jax.nn.silu(g_ret)

        mixed = jnp.concatenate([y_lru, y_ret], axis=-1)
        h = h + mixed @ w_out[l].astype(dt)

        u = rms_norm(h, norm2_gain[l])
        up = causal_depthwise_conv(u @ ffn_up_w[l].astype(dt), ffn_conv_w[l], ffn_conv_b[l])
        a_branch, v_branch = jnp.split(up, 2, axis=-1)
        h = h + (jax.nn.gelu(a_branch) * v_branch) @ ffn_down_w[l].astype(dt)

    return rms_norm(h, final_norm_gain)
```

```python
import functools
import math

import numpy as np
import jax
import jax.numpy as jnp
from jax import lax
from jax.experimental import pallas as pl
from jax.experimental.pallas import tpu as pltpu

D_MODEL = 1024
D_LRU = 512
LRU_HEADS = 8
LRU_HEAD_DIM = D_LRU // LRU_HEADS
LRU_CONV = 4
LRU_C = 8.0
D_RET = 512
RET_HEADS = 4
RET_HEAD_DIM = D_RET // RET_HEADS
RET_CHUNK = 128
ROPE_BASE = 10000.0
D_IN = 2 * D_LRU + 4 * D_RET
D_FF = 3 * D_MODEL
FFN_CONV = 3
NORM_EPS = 1e-6

SUBLANES = 8
LANES = 128
MXU_DIM = 256
VMEM_LIMIT_BYTES = 56 * 1024 * 1024

SEQ_TILE = 512
FF_CHUNK = 512
GATE_GROUP = MXU_DIM

BF16 = jnp.bfloat16
F32 = jnp.float32


def _rms(x, gain):
    ms = jnp.mean(x * x, axis=-1, keepdims=True)
    return x * lax.rsqrt(ms + NORM_EPS) * gain


def _gelu_tanh(x):
    c = math.sqrt(2.0 / math.pi)
    cdf = 0.5 * (1.0 + jnp.tanh(c * (x + 0.044715 * (x * x * x))))
    return x * cdf


def _softplus(x):
    return jnp.maximum(x, 0.0) + jnp.log1p(jnp.exp(-jnp.abs(x)))


def _dot(a, b):
    return jnp.dot(a, b, preferred_element_type=F32)


def _mixer_kernel(x_ref, cos_ref, sin_ref, g1_ref, w_in_ref, convw_ref, convb_ref,
                  wg_ref, ba_ref, bx_ref, lam_ref, lrug_ref, retg_ref, w_out_ref,
                  decay_ref, xi_ref, zeta_ref, out_ref,
                  xl_buf, a_buf, h_buf, h_carry, q_buf, k_buf, v_buf, o_buf, state,
                  *, ts, g_chunk):
    s = pl.program_id(1)

    @pl.when(s == 0)
    def _():
        xl_buf[0:SUBLANES, :] = jnp.zeros((SUBLANES, D_LRU), F32)
        h_carry[...] = jnp.zeros_like(h_carry)
        state[...] = jnp.zeros_like(state)

    x = x_ref[...]
    ub = _rms(x, g1_ref[...]).astype(BF16)

    xl = _dot(ub, w_in_ref[:, 0:D_LRU])
    xl_buf[SUBLANES:SUBLANES + ts, :] = xl
    cw = convw_ref[...]
    xc = (convb_ref[...]
          + cw[3:4, :] * xl
          + cw[2:3, :] * xl_buf[SUBLANES - 1:SUBLANES - 1 + ts, :]
          + cw[1:2, :] * xl_buf[SUBLANES - 2:SUBLANES - 2 + ts, :]
          + cw[0:1, :] * xl_buf[SUBLANES - 3:SUBLANES - 3 + ts, :])
    xl_buf[0:SUBLANES, :] = xl_buf[ts:ts + SUBLANES, :]
    xcb = xc.astype(BF16)

    neg_c_sp = -LRU_C * _softplus(-lam_ref[...])
    for g in range(D_LRU // GATE_GROUP):
        cs = slice(g * GATE_GROUP, (g + 1) * GATE_GROUP)
        gates = _dot(xcb[:, cs], wg_ref[g])
        r = jax.nn.sigmoid(gates[:, :GATE_GROUP] + ba_ref[:, cs])
        i = jax.nn.sigmoid(gates[:, GATE_GROUP:] + bx_ref[:, cs])
        log_a = neg_c_sp[:, cs] * r
        a = jnp.exp(log_a)
        t = jnp.tanh(log_a)
        mult = jnp.sqrt(-2.0 * t / (1.0 - t))
        a_buf[:, cs] = a
        h_buf[:, cs] = mult * (i * xc[:, cs])

    row = lax.broadcasted_iota(jnp.int32, (SUBLANES, D_LRU), 0)

    def scan_block(j, hprev):
        r0 = pl.multiple_of(j * SUBLANES, SUBLANES)
        a_blk = a_buf[pl.ds(r0, SUBLANES), :]
        b_blk = h_buf[pl.ds(r0, SUBLANES), :]
        for k in (1, 2, 4):
            keep = row >= k
            a_sh = pltpu.roll(a_blk, k, axis=0)
            b_sh = pltpu.roll(b_blk, k, axis=0)
            b_blk = jnp.where(keep, a_blk * b_sh + b_blk, b_blk)
            a_blk = jnp.where(keep, a_blk * a_sh, a_blk)
        h = a_blk * hprev + b_blk
        h_buf[pl.ds(r0, SUBLANES), :] = h
        return jnp.broadcast_to(h[SUBLANES - 1:SUBLANES, :], (SUBLANES, D_LRU))

    h_carry[...] = lax.fori_loop(0, ts // SUBLANES, scan_block, h_carry[...], unroll=4)

    g_lru = _dot(ub, w_in_ref[:, D_LRU:2 * D_LRU])
    y_lru = _rms(h_buf[...] * _gelu_tanh(g_lru), lrug_ref[...])

    q0 = 2 * D_LRU
    q = _dot(ub, w_in_ref[:, q0:q0 + D_RET])
    k = _dot(ub, w_in_ref[:, q0 + D_RET:q0 + 2 * D_RET])
    v_buf[...] = _dot(ub, w_in_ref[:, q0 + 2 * D_RET:q0 + 3 * D_RET]).astype(BF16)
    cosf = cos_ref[...]
    sinf = sin_ref[...]
    half = RET_HEAD_DIM // 2
    for h in range(RET_HEADS):
        hs = slice(h * RET_HEAD_DIM, (h + 1) * RET_HEAD_DIM)
        qh = q[:, hs]
        kh = k[:, hs]
        q_buf[:, hs] = qh * cosf + pltpu.roll(qh, half, axis=1) * sinf
        k_buf[:, hs] = kh * cosf + pltpu.roll(kh, half, axis=1) * sinf

    for c in range(ts // RET_CHUNK):
        rs = slice(c * RET_CHUNK, (c + 1) * RET_CHUNK)
        qc = q_buf[rs, :]
        kc = k_buf[rs, :]
        qx = (qc * xi_ref[...]).astype(BF16)
        kz = (kc * zeta_ref[...]).astype(BF16)
        qcb = qc.astype(BF16)
        kcb = kc.astype(BF16)
        vc = v_buf[rs, :]
        for h in range(RET_HEADS):
            hs = slice(h * RET_HEAD_DIM, (h + 1) * RET_HEAD_DIM)
            scores = lax.dot_general(qcb[:, hs], kcb[:, hs], (((1,), (1,)), ((), ())),
                                     preferred_element_type=F32)
            p = (scores * decay_ref[h]).astype(BF16)
            st = state[h]
            inner = _dot(p, vc[:, hs])
            cross = _dot(qx[:, hs], st.astype(BF16))
            o_buf[rs, hs] = inner + cross
            kv = lax.dot_general(kz[:, hs], vc[:, hs], (((0,), (0,)), ((), ())),
                                 preferred_element_type=F32)
            state[h] = st * g_chunk[h] + kv

    g_ret = _dot(ub, w_in_ref[:, q0 + 3 * D_RET:q0 + 4 * D_RET])
    gate = g_ret * jax.nn.sigmoid(g_ret)
    y_parts = [y_lru.astype(BF16)]
    for h in range(RET_HEADS):
        hs = slice(h * RET_HEAD_DIM, (h + 1) * RET_HEAD_DIM)
        o = o_buf[:, hs]
        mu = jnp.mean(o, axis=-1, keepdims=True)
        d = o - mu
        var = jnp.mean(d * d, axis=-1, keepdims=True)
        y = d * lax.rsqrt(var + NORM_EPS) * retg_ref[:, hs]
        y_parts.append((y * gate[:, hs]).astype(BF16))
    mixed = jnp.concatenate(y_parts, axis=1)
    out_ref[...] = x + _dot(mixed, w_out_ref[...])


def _ffn_kernel(x_ref, g2_ref, w_up_ref, convw_ref, convb_ref, w_down_ref, gf_ref, out_ref,
                wa_buf, wv_buf, carry, *, ts):
    s = pl.program_id(1)

    @pl.when(s == 0)
    def _():
        carry[...] = jnp.zeros_like(carry)

    x = x_ref[...]
    ub = _rms(x, g2_ref[...]).astype(BF16)
    acc = x
    n_chunks = D_FF // FF_CHUNK
    for j in range(n_chunks):
        branches = []
        for br, buf in ((0, wa_buf), (1, wv_buf)):
            c0 = br * D_FF + j * FF_CHUNK
            cs = slice(c0, c0 + FF_CHUNK)
            up = _dot(ub, w_up_ref[:, cs])
            buf[0:SUBLANES, :] = carry[br * n_chunks + j]
            buf[SUBLANES:SUBLANES + ts, :] = up
            cw = convw_ref[:, cs]
            conv = (convb_ref[:, cs]
                    + cw[2:3, :] * up
                    + cw[1:2, :] * buf[SUBLANES - 1:SUBLANES - 1 + ts, :]
                    + cw[0:1, :] * buf[SUBLANES - 2:SUBLANES - 2 + ts, :])
            carry[br * n_chunks + j] = buf[ts:ts + SUBLANES, :]
            branches.append(conv)
        gated = (_gelu_tanh(branches[0]) * branches[1]).astype(BF16)
        acc = acc + _dot(gated, w_down_ref[j * FF_CHUNK:(j + 1) * FF_CHUNK, :])
    out_ref[...] = _rms(acc, gf_ref[...])


def _const_spec(shape):
    n = len(shape)
    return pl.BlockSpec(shape, lambda b, s: (0,) * n, pipeline_mode=pl.Buffered(1))


def _block_diag(w):
    heads, d, _ = w.shape
    eye = jnp.eye(heads, dtype=w.dtype)
    return jnp.einsum("hij,hg->higj", w, eye).reshape(heads * d, heads * d)


def _retention_tables():
    c = RET_CHUNK
    log_g = np.log1p(-np.exp2(-5.0 - np.arange(RET_HEADS, dtype=np.float64)))
    idx = np.arange(c, dtype=np.float64)
    diff = idx[:, None] - idx[None, :]
    scale = RET_HEAD_DIM ** -0.5
    decay = np.where(diff[None] >= 0, np.exp(np.maximum(diff, 0.0)[None] * log_g[:, None, None]), 0.0)
    zeta = np.exp((c - 1 - idx)[None, :] * log_g[:, None])
    xi = np.exp((idx + 1.0)[None, :] * log_g[:, None])
    g_chunk = tuple(float(np.float32(np.exp(c * lg))) for lg in log_g)
    decay = (decay * scale).astype(np.float32)
    zeta_tab = np.repeat((zeta * scale).T, RET_HEAD_DIM, axis=1).astype(np.float32)
    xi_tab = np.repeat(xi.T, RET_HEAD_DIM, axis=1).astype(np.float32)
    return decay, xi_tab, zeta_tab, g_chunk


def _rotary_tables(slen):
    pos = jnp.arange(slen, dtype=F32)
    inv_freq = ROPE_BASE ** (-jnp.arange(0, RET_HEAD_DIM, 2, dtype=F32) / RET_HEAD_DIM)
    ang = pos[:, None] * inv_freq[None, :]
    cos = jnp.cos(ang)
    sin = jnp.sin(ang)
    return jnp.concatenate([cos, cos], axis=1), jnp.concatenate([-sin, sin], axis=1)


def kernel(x, norm1_gain, w_in, lru_conv_w, lru_conv_b, lru_gate_a_w, lru_gate_a_b,
           lru_gate_x_w, lru_gate_x_b, lru_lambda, lru_norm_gain, ret_norm_gain, w_out,
           norm2_gain, ffn_up_w, ffn_conv_w, ffn_conv_b, ffn_down_w, final_norm_gain):
    bsz, slen, d_model = x.shape
    depth = w_in.shape[0]
    assert d_model == D_MODEL and slen % SEQ_TILE == 0 and SEQ_TILE % RET_CHUNK == 0
    ts = SEQ_TILE
    grid = (bsz, slen // ts)
    cosf, sinf = _rotary_tables(slen)
    decay, xi_tab, zeta_tab, g_chunk = _retention_tables()
    params = pltpu.CompilerParams(dimension_semantics=("arbitrary", "arbitrary"),
                                  vmem_limit_bytes=VMEM_LIMIT_BYTES)
    tile_spec = pl.BlockSpec((None, ts, D_MODEL), lambda b, s: (b, s, 0))
    rot_spec = pl.BlockSpec((ts, RET_HEAD_DIM), lambda b, s: (s, 0))
    row = lambda a: a.reshape(1, -1).astype(F32)

    h = x
    for l in range(depth):
        wa = _block_diag(lru_gate_a_w[l])
        wx = _block_diag(lru_gate_x_w[l])
        n_g = D_LRU // GATE_GROUP
        wg = jnp.stack([
            jnp.concatenate([wa[g * GATE_GROUP:(g + 1) * GATE_GROUP, g * GATE_GROUP:(g + 1) * GATE_GROUP],
                             wx[g * GATE_GROUP:(g + 1) * GATE_GROUP, g * GATE_GROUP:(g + 1) * GATE_GROUP]],
                            axis=1)
            for g in range(n_g)]).astype(BF16)

        mixer = pl.pallas_call(
            functools.partial(_mixer_kernel, ts=ts, g_chunk=g_chunk),
            name="token_mixer",
            grid=grid,
            in_specs=[
                tile_spec, rot_spec, rot_spec,
                _const_spec((1, D_MODEL)),
                _const_spec((D_MODEL, D_IN)),
                _const_spec((LRU_CONV, D_LRU)),
                _const_spec((1, D_LRU)),
                _const_spec((n_g, GATE_GROUP, 2 * GATE_GROUP)),
                _const_spec((1, D_LRU)), _const_spec((1, D_LRU)), _const_spec((1, D_LRU)),
                _const_spec((1, D_LRU)), _const_spec((1, D_RET)),
                _const_spec((D_MODEL, D_MODEL)),
                _const_spec((RET_HEADS, RET_CHUNK, RET_CHUNK)),
                _const_spec((RET_CHUNK, D_RET)), _const_spec((RET_CHUNK, D_RET)),
            ],
            out_specs=tile_spec,
            out_shape=jax.ShapeDtypeStruct((bsz, slen, D_MODEL), F32),
            scratch_shapes=[
                pltpu.VMEM((ts + SUBLANES, D_LRU), F32),
                pltpu.VMEM((ts, D_LRU), F32),
                pltpu.VMEM((ts, D_LRU), F32),
                pltpu.VMEM((SUBLANES, D_LRU), F32),
                pltpu.VMEM((ts, D_RET), F32),
                pltpu.VMEM((ts, D_RET), F32),
                pltpu.VMEM((ts, D_RET), BF16),
                pltpu.VMEM((ts, D_RET), F32),
                pltpu.VMEM((RET_HEADS, RET_HEAD_DIM, RET_HEAD_DIM), F32),
            ],
            compiler_params=params,
        )
        h = mixer(h, cosf, sinf, row(norm1_gain[l]), w_in[l].astype(BF16),
                  lru_conv_w[l].astype(F32), row(lru_conv_b[l]), wg,
                  row(lru_gate_a_b[l]), row(lru_gate_x_b[l]), row(lru_lambda[l]),
                  row(lru_norm_gain[l]), row(ret_norm_gain[l]), w_out[l].astype(BF16),
                  jnp.asarray(decay), jnp.asarray(xi_tab), jnp.asarray(zeta_tab))

        last = l == depth - 1
        gf = row(final_norm_gain) if last else None
        assert last, "final norm is fused into the last layer's channel mixer"
        n_carry = 2 * (D_FF // FF_CHUNK)
        ffn = pl.pallas_call(
            functools.partial(_ffn_kernel, ts=ts),
            name="channel_mixer",
            grid=grid,
            in_specs=[
                tile_spec,
                _const_spec((1, D_MODEL)),
                _const_spec((D_MODEL, 2 * D_FF)),
                _const_spec((FFN_CONV, 2 * D_FF)),
                _const_spec((1, 2 * D_FF)),
                _const_spec((D_FF, D_MODEL)),
                _const_spec((1, D_MODEL)),
            ],
            out_specs=tile_spec,
            out_shape=jax.ShapeDtypeStruct((bsz, slen, D_MODEL), F32),
            scratch_shapes=[
                pltpu.VMEM((ts + SUBLANES, FF_CHUNK), F32),
                pltpu.VMEM((ts + SUBLANES, FF_CHUNK), F32),
                pltpu.VMEM((n_carry, SUBLANES, FF_CHUNK), F32),
            ],
            compiler_params=params,
        )
        h = ffn(h, row(norm2_gain[l]), ffn_up_w[l].astype(BF16), ffn_conv_w[l].astype(F32),
                row(ffn_conv_b[l]), ffn_down_w[l].astype(BF16), gf)
    return h
```

```python
import functools
import math

import numpy as np
import jax
import jax.numpy as jnp
from jax import lax
from jax.experimental import pallas as pl
from jax.experimental.pallas import tpu as pltpu

D_MODEL = 1024
D_LRU = 512
LRU_HEADS = 8
LRU_HEAD_DIM = D_LRU // LRU_HEADS
LRU_CONV = 4
LRU_C = 8.0
D_RET = 512
RET_HEADS = 4
RET_HEAD_DIM = D_RET // RET_HEADS
RET_CHUNK = 128
ROPE_BASE = 10000.0
D_IN = 2 * D_LRU + 4 * D_RET
D_FF = 3 * D_MODEL
FFN_CONV = 3
NORM_EPS = 1e-6

SUBLANES = 8
LANES = 128
MXU_DIM = 256
VMEM_LIMIT_BYTES = 56 * 1024 * 1024

SEQ_TILE = 512
RET_BLOCK = 256
FF_CHUNK = 512
GATE_GROUP = MXU_DIM

BF16 = jnp.bfloat16
F32 = jnp.float32


def _rms(x, gain):
    ms = jnp.mean(x * x, axis=-1, keepdims=True)
    return x * lax.rsqrt(ms + NORM_EPS) * gain


def _gelu_tanh(x):
    c = math.sqrt(2.0 / math.pi)
    cdf = 0.5 * (1.0 + jnp.tanh(c * (x + 0.044715 * (x * x * x))))
    return x * cdf


def _softplus(x):
    return jnp.maximum(x, 0.0) + jnp.log1p(jnp.exp(-jnp.abs(x)))


def _dot(a, b):
    return jnp.dot(a, b, preferred_element_type=F32)


def _pack_rows(w):
    k, n = w.shape
    pairs = w.astype(BF16).reshape(k // 2, 2, n).swapaxes(1, 2)
    return lax.bitcast_convert_type(pairs, jnp.uint32)


def _wdot(a, w_ref, r0=None, r1=None, c0=None, c1=None, lead=None):
    rs = slice(None) if r0 is None else slice(r0 // 2, r1 // 2)
    cs = slice(None) if c0 is None else slice(c0, c1)
    packed = w_ref[rs, cs] if lead is None else w_ref[lead, rs, cs]
    return _dot(a, pltpu.bitcast(packed, BF16))


def _mixer_kernel(x_ref, cos_ref, sin_ref, g1_ref, w_in_ref, convw_ref, convb_ref,
                  wg_ref, ba_ref, bx_ref, lam_ref, lrug_ref, retg_ref, w_out_ref,
                  decay_ref, xi_ref, zeta_ref, out_ref,
                  xl_buf, a_buf, h_buf, h_carry, q_buf, k_buf, v_buf, o_buf, state,
                  *, ts, g_chunk):
    s = pl.program_id(1)

    @pl.when(s == 0)
    def _():
        xl_buf[0:SUBLANES, :] = jnp.zeros((SUBLANES, D_LRU), F32)
        h_carry[...] = jnp.zeros_like(h_carry)
        state[...] = jnp.zeros_like(state)

    x = x_ref[...]
    ub = _rms(x, g1_ref[...]).astype(BF16)

    xl = _wdot(ub, w_in_ref, c0=0, c1=D_LRU)
    xl_buf[SUBLANES:SUBLANES + ts, :] = xl
    cw = convw_ref[...]
    xc = (convb_ref[...]
          + cw[3:4, :] * xl
          + cw[2:3, :] * xl_buf[SUBLANES - 1:SUBLANES - 1 + ts, :]
          + cw[1:2, :] * xl_buf[SUBLANES - 2:SUBLANES - 2 + ts, :]
          + cw[0:1, :] * xl_buf[SUBLANES - 3:SUBLANES - 3 + ts, :])
    xl_buf[0:SUBLANES, :] = xl_buf[ts:ts + SUBLANES, :]
    xcb = xc.astype(BF16)

    neg_c_sp = -LRU_C * _softplus(-lam_ref[...])
    for g in range(D_LRU // GATE_GROUP):
        cs = slice(g * GATE_GROUP, (g + 1) * GATE_GROUP)
        gates = _wdot(xcb[:, cs], wg_ref, lead=g)
        r = jax.nn.sigmoid(gates[:, :GATE_GROUP] + ba_ref[:, cs])
        i = jax.nn.sigmoid(gates[:, GATE_GROUP:] + bx_ref[:, cs])
        log_a = neg_c_sp[:, cs] * r
        a = jnp.exp(log_a)
        t = jnp.tanh(log_a)
        mult = jnp.sqrt(-2.0 * t / (1.0 - t))
        a_buf[:, cs] = a
        h_buf[:, cs] = mult * (i * xc[:, cs])

    row = lax.broadcasted_iota(jnp.int32, (SUBLANES, D_LRU), 0)

    def scan_block(j, hprev):
        r0 = pl.multiple_of(j * SUBLANES, SUBLANES)
        a_blk = a_buf[pl.ds(r0, SUBLANES), :]
        b_blk = h_buf[pl.ds(r0, SUBLANES), :]
        for k in (1, 2, 4):
            keep = row >= k
            a_sh = pltpu.roll(a_blk, k, axis=0)
            b_sh = pltpu.roll(b_blk, k, axis=0)
            b_blk = jnp.where(keep, a_blk * b_sh + b_blk, b_blk)
            a_blk = jnp.where(keep, a_blk * a_sh, a_blk)
        h = a_blk * hprev + b_blk
        h_buf[pl.ds(r0, SUBLANES), :] = h
        return jnp.broadcast_to(h[SUBLANES - 1:SUBLANES, :], (SUBLANES, D_LRU))

    h_carry[...] = lax.fori_loop(0, ts // SUBLANES, scan_block, h_carry[...], unroll=True)

    g_lru = _wdot(ub, w_in_ref, c0=D_LRU, c1=2 * D_LRU)
    y_lru = _rms(h_buf[...] * _gelu_tanh(g_lru), lrug_ref[...])

    q0 = 2 * D_LRU
    q = _wdot(ub, w_in_ref, c0=q0, c1=q0 + D_RET)
    k = _wdot(ub, w_in_ref, c0=q0 + D_RET, c1=q0 + 2 * D_RET)
    v_buf[...] = _wdot(ub, w_in_ref, c0=q0 + 2 * D_RET, c1=q0 + 3 * D_RET).astype(BF16)
    cosf = cos_ref[...]
    sinf = sin_ref[...]
    half = RET_HEAD_DIM // 2
    for h in range(RET_HEADS):
        hs = slice(h * RET_HEAD_DIM, (h + 1) * RET_HEAD_DIM)
        qh = q[:, hs]
        kh = k[:, hs]
        q_buf[:, hs] = qh * cosf + pltpu.roll(qh, half, axis=1) * sinf
        k_buf[:, hs] = kh * cosf + pltpu.roll(kh, half, axis=1) * sinf

    for c in range(ts // RET_BLOCK):
        rs = slice(c * RET_BLOCK, (c + 1) * RET_BLOCK)
        qc = q_buf[rs, :]
        kc = k_buf[rs, :]
        qx = (qc * xi_ref[...]).astype(BF16)
        kz = (kc * zeta_ref[...]).astype(BF16)
        qcb = qc.astype(BF16)
        kcb = kc.astype(BF16)
        vc = v_buf[rs, :]
        for h in range(RET_HEADS):
            hs = slice(h * RET_HEAD_DIM, (h + 1) * RET_HEAD_DIM)
            scores = lax.dot_general(qcb[:, hs], kcb[:, hs], (((1,), (1,)), ((), ())),
                                     preferred_element_type=F32)
            p = (scores * decay_ref[h]).astype(BF16)
            st = state[h]
            lhs = jnp.concatenate([p, qx[:, hs]], axis=1)
            rhs = jnp.concatenate([vc[:, hs], st.astype(BF16)], axis=0)
            o_buf[rs, hs] = _dot(lhs, rhs)
            kv = lax.dot_general(kz[:, hs], vc[:, hs], (((0,), (0,)), ((), ())),
                                 preferred_element_type=F32)
            state[h] = st * g_chunk[h] + kv

    g_ret = _wdot(ub, w_in_ref, c0=q0 + 3 * D_RET, c1=q0 + 4 * D_RET)
    gate = g_ret * jax.nn.sigmoid(g_ret)
    y_parts = [y_lru.astype(BF16)]
    for h in range(RET_HEADS):
        hs = slice(h * RET_HEAD_DIM, (h + 1) * RET_HEAD_DIM)
        o = o_buf[:, hs]
        mu = jnp.mean(o, axis=-1, keepdims=True)
        d = o - mu
        var = jnp.mean(d * d, axis=-1, keepdims=True)
        y = d * lax.rsqrt(var + NORM_EPS) * retg_ref[:, hs]
        y_parts.append((y * gate[:, hs]).astype(BF16))
    mixed = jnp.concatenate(y_parts, axis=1)
    out_ref[...] = x + _wdot(mixed, w_out_ref)


def _gelu_times(x, v):
    c = math.sqrt(2.0 / math.pi)
    t = jnp.tanh(x * (c + (0.044715 * c) * (x * x)))
    return (0.5 * x * v) * (1.0 + t)


def _ffn_kernel(x_ref, g2_ref, w_up_ref, convw_ref, convb_ref, w_down_ref, gf_ref, out_ref,
                o_buf, *up_bufs, ts):
    s = pl.program_id(1)

    @pl.when(s == 0)
    def _():
        for buf in up_bufs:
            for i in range(buf.shape[0]):
                buf[i, 0:SUBLANES, :] = jnp.zeros((SUBLANES, LANES), F32)

    half = ts // 2
    n_chunks = D_FF // FF_CHUNK
    n_slab = FF_CHUNK // LANES
    x = x_ref[...]
    ub = _rms(x, g2_ref[...]).astype(BF16)
    acc = None

    def up_dot(j):
        return _wdot(ub, w_up_ref, c0=j * 2 * FF_CHUNK, c1=(j + 1) * 2 * FF_CHUNK)

    ups = [up_dot(j) for j in range(n_chunks)]
    for j in range(n_chunks):
        c0 = j * 2 * FF_CHUNK
        up = ups[j]
        conv = []
        for k in range(2 * n_slab):
            cs = slice(c0 + k * LANES, c0 + (k + 1) * LANES)
            buf = up_bufs[j]
            buf[k, SUBLANES:SUBLANES + ts, :] = up[:, k * LANES:(k + 1) * LANES]
            x_e = buf[k, pl.ds(SUBLANES, half, stride=2), :]
            x_o = buf[k, pl.ds(SUBLANES + 1, half, stride=2), :]
            x_om = buf[k, pl.ds(SUBLANES - 1, half, stride=2), :]
            x_em = buf[k, pl.ds(SUBLANES - 2, half, stride=2), :]
            buf[k, 0:SUBLANES, :] = buf[k, ts:ts + SUBLANES, :]
            w0 = jnp.broadcast_to(convw_ref[0:1, cs], (half, LANES))
            w1 = jnp.broadcast_to(convw_ref[1:2, cs], (half, LANES))
            w2 = jnp.broadcast_to(convw_ref[2:3, cs], (half, LANES))
            b = jnp.broadcast_to(convb_ref[:, cs], (half, LANES))
            y_e = b + w2 * x_e + w1 * x_om + w0 * x_em
            y_o = b + w2 * x_o + w1 * x_e + w0 * x_om
            conv.append((y_e, y_o))
        gated = jnp.concatenate(
            [jnp.concatenate([_gelu_times(conv[k][p], conv[k + n_slab][p]) for p in (0, 1)], axis=0)
             for k in range(n_slab)], axis=1).astype(BF16)
        d = _wdot(gated, w_down_ref, r0=j * FF_CHUNK, r1=(j + 1) * FF_CHUNK)
        acc = d if acc is None else acc + d
    for k in range(D_MODEL // LANES):
        o_buf[k, pl.ds(0, half, stride=2), :] = acc[0:half, k * LANES:(k + 1) * LANES]
        o_buf[k, pl.ds(1, half, stride=2), :] = acc[half:ts, k * LANES:(k + 1) * LANES]
    y = jnp.concatenate([o_buf[k] for k in range(D_MODEL // LANES)], axis=1)
    out_ref[...] = _rms(x + y, gf_ref[...])


def _const_spec(shape):
    n = len(shape)
    return pl.BlockSpec(shape, lambda b, s: (0,) * n, pipeline_mode=pl.Buffered(1))


def _block_diag(w):
    heads, d, _ = w.shape
    eye = jnp.eye(heads, dtype=w.dtype)
    return jnp.einsum("hij,hg->higj", w, eye).reshape(heads * d, heads * d)


def _retention_tables():
    c = RET_BLOCK
    log_g = np.log1p(-np.exp2(-5.0 - np.arange(RET_HEADS, dtype=np.float64)))
    idx = np.arange(c, dtype=np.float64)
    diff = idx[:, None] - idx[None, :]
    scale = RET_HEAD_DIM ** -0.5
    decay = np.where(diff[None] >= 0, np.exp(np.maximum(diff, 0.0)[None] * log_g[:, None, None]), 0.0)
    zeta = np.exp((c - 1 - idx)[None, :] * log_g[:, None])
    xi = np.exp((idx + 1.0)[None, :] * log_g[:, None])
    g_chunk = tuple(float(np.float32(np.exp(c * lg))) for lg in log_g)
    decay = (decay * scale).astype(np.float32)
    zeta_tab = np.repeat((zeta * scale).T, RET_HEAD_DIM, axis=1).astype(np.float32)
    xi_tab = np.repeat(xi.T, RET_HEAD_DIM, axis=1).astype(np.float32)
    return decay, xi_tab, zeta_tab, g_chunk


def _rotary_tables(slen):
    pos = jnp.arange(slen, dtype=F32)
    inv_freq = ROPE_BASE ** (-jnp.arange(0, RET_HEAD_DIM, 2, dtype=F32) / RET_HEAD_DIM)
    ang = pos[:, None] * inv_freq[None, :]
    cos = jnp.cos(ang)
    sin = jnp.sin(ang)
    return jnp.concatenate([cos, cos], axis=1), jnp.concatenate([-sin, sin], axis=1)


def kernel(x, norm1_gain, w_in, lru_conv_w, lru_conv_b, lru_gate_a_w, lru_gate_a_b,
           lru_gate_x_w, lru_gate_x_b, lru_lambda, lru_norm_gain, ret_norm_gain, w_out,
           norm2_gain, ffn_up_w, ffn_conv_w, ffn_conv_b, ffn_down_w, final_norm_gain):
    bsz, slen, d_model = x.shape
    depth = w_in.shape[0]
    assert d_model == D_MODEL and slen % SEQ_TILE == 0 and SEQ_TILE % RET_BLOCK == 0
    ts = SEQ_TILE
    grid = (bsz, slen // ts)
    cosf, sinf = _rotary_tables(slen)
    decay, xi_tab, zeta_tab, g_chunk = _retention_tables()
    params = pltpu.CompilerParams(dimension_semantics=("arbitrary", "arbitrary"),
                                  vmem_limit_bytes=VMEM_LIMIT_BYTES)
    tile_spec = pl.BlockSpec((None, ts, D_MODEL), lambda b, s: (b, s, 0))
    rot_spec = pl.BlockSpec((ts, RET_HEAD_DIM), lambda b, s: (s, 0))
    row = lambda a: a.reshape(1, -1).astype(F32)

    h = x
    for l in range(depth):
        wa = _block_diag(lru_gate_a_w[l])
        wx = _block_diag(lru_gate_x_w[l])
        n_g = D_LRU // GATE_GROUP
        wg = jnp.stack([
            jnp.concatenate([wa[g * GATE_GROUP:(g + 1) * GATE_GROUP, g * GATE_GROUP:(g + 1) * GATE_GROUP],
                             wx[g * GATE_GROUP:(g + 1) * GATE_GROUP, g * GATE_GROUP:(g + 1) * GATE_GROUP]],
                            axis=1)
            for g in range(n_g)])
        wg = jnp.stack([_pack_rows(wg[g]) for g in range(n_g)])

        mixer = pl.pallas_call(
            functools.partial(_mixer_kernel, ts=ts, g_chunk=g_chunk),
            name="token_mixer",
            grid=grid,
            in_specs=[
                tile_spec, rot_spec, rot_spec,
                _const_spec((1, D_MODEL)),
                _const_spec((D_MODEL // 2, D_IN)),
                _const_spec((LRU_CONV, D_LRU)),
                _const_spec((1, D_LRU)),
                _const_spec((n_g, GATE_GROUP // 2, 2 * GATE_GROUP)),
                _const_spec((1, D_LRU)), _const_spec((1, D_LRU)), _const_spec((1, D_LRU)),
                _const_spec((1, D_LRU)), _const_spec((1, D_RET)),
                _const_spec((D_MODEL // 2, D_MODEL)),
                _const_spec((RET_HEADS, RET_BLOCK, RET_BLOCK)),
                _const_spec((RET_BLOCK, D_RET)), _const_spec((RET_BLOCK, D_RET)),
            ],
            out_specs=tile_spec,
            out_shape=jax.ShapeDtypeStruct((bsz, slen, D_MODEL), F32),
            scratch_shapes=[
                pltpu.VMEM((ts + SUBLANES, D_LRU), F32),
                pltpu.VMEM((ts, D_LRU), F32),
                pltpu.VMEM((ts, D_LRU), F32),
                pltpu.VMEM((SUBLANES, D_LRU), F32),
                pltpu.VMEM((ts, D_RET), F32),
                pltpu.VMEM((ts, D_RET), F32),
                pltpu.VMEM((ts, D_RET), BF16),
                pltpu.VMEM((ts, D_RET), F32),
                pltpu.VMEM((RET_HEADS, RET_HEAD_DIM, RET_HEAD_DIM), F32),
            ],
            compiler_params=params,
        )
        h = mixer(h, cosf, sinf, row(norm1_gain[l]), _pack_rows(w_in[l]),
                  lru_conv_w[l].astype(F32), row(lru_conv_b[l]), wg,
                  row(lru_gate_a_b[l]), row(lru_gate_x_b[l]), row(lru_lambda[l]),
                  row(lru_norm_gain[l]), row(ret_norm_gain[l]), _pack_rows(w_out[l]),
                  jnp.asarray(decay), jnp.asarray(xi_tab), jnp.asarray(zeta_tab))

        last = l == depth - 1
        gf = row(final_norm_gain) if last else None
        assert last, "final norm is fused into the last layer's channel mixer"
        n_chunks = D_FF // FF_CHUNK
        n_slab2 = 2 * FF_CHUNK // LANES

        def chunked(a):
            r = a.shape[0]
            return a.reshape(r, 2, n_chunks, FF_CHUNK).transpose(0, 2, 1, 3).reshape(r, 2 * D_FF)

        ffn = pl.pallas_call(
            functools.partial(_ffn_kernel, ts=ts),
            name="channel_mixer",
            grid=grid,
            in_specs=[
                tile_spec,
                _const_spec((1, D_MODEL)),
                _const_spec((D_MODEL // 2, 2 * D_FF)),
                _const_spec((FFN_CONV, 2 * D_FF)),
                _const_spec((1, 2 * D_FF)),
                _const_spec((D_FF // 2, D_MODEL)),
                _const_spec((1, D_MODEL)),
            ],
            out_specs=tile_spec,
            out_shape=jax.ShapeDtypeStruct((bsz, slen, D_MODEL), F32),
            scratch_shapes=[
                pltpu.VMEM((D_MODEL // LANES, ts, LANES), F32),
            ] + [pltpu.VMEM((n_slab2, ts + SUBLANES, LANES), F32)] * n_chunks,
            compiler_params=params,
        )
        h = ffn(h, row(norm2_gain[l]), _pack_rows(chunked(ffn_up_w[l])),
                chunked(ffn_conv_w[l].astype(F32)), chunked(row(ffn_conv_b[l])),
                _pack_rows(ffn_down_w[l]), gf)
    return h
```

```python
import functools
import math

import numpy as np
import jax
import jax.numpy as jnp
from jax import lax
from jax.experimental import pallas as pl
from jax.experimental.pallas import tpu as pltpu

D_MODEL = 1024
D_LRU = 512
LRU_HEADS = 8
LRU_HEAD_DIM = D_LRU // LRU_HEADS
LRU_CONV = 4
LRU_C = 8.0
D_RET = 512
RET_HEADS = 4
RET_HEAD_DIM = D_RET // RET_HEADS
RET_CHUNK = 128
ROPE_BASE = 10000.0
D_IN = 2 * D_LRU + 4 * D_RET
D_FF = 3 * D_MODEL
FFN_CONV = 3
NORM_EPS = 1e-6

SUBLANES = 8
LANES = 128
MXU_DIM = 256
VMEM_LIMIT_BYTES = 56 * 1024 * 1024

SEQ_TILE = 512
RET_BLOCK = 256
FF_CHUNK = 512
GATE_GROUP = MXU_DIM

BF16 = jnp.bfloat16
F32 = jnp.float32


def _rms(x, gain):
    ms = jnp.mean(x * x, axis=-1, keepdims=True)
    return x * lax.rsqrt(ms + NORM_EPS) * gain


def _gelu_tanh(x):
    c = math.sqrt(2.0 / math.pi)
    cdf = 0.5 * (1.0 + jnp.tanh(c * (x + 0.044715 * (x * x * x))))
    return x * cdf


def _softplus(x):
    return jnp.maximum(x, 0.0) + jnp.log1p(jnp.exp(-jnp.abs(x)))


def _dot(a, b):
    return jnp.dot(a, b, preferred_element_type=F32)


def _pack_rows(w):
    bits = lax.bitcast_convert_type(w.astype(BF16), jnp.uint16).astype(jnp.uint32)
    return bits[0::2] | (bits[1::2] << 16)


def _wdot(a, w_ref, r0=None, r1=None, c0=None, c1=None, lead=None):
    rs = slice(None) if r0 is None else slice(r0 // 2, r1 // 2)
    cs = slice(None) if c0 is None else slice(c0, c1)
    packed = w_ref[rs, cs] if lead is None else w_ref[lead, rs, cs]
    return _dot(a, pltpu.bitcast(packed, BF16))


def _mixer_kernel(x_ref, cos_ref, sin_ref, g1_ref, w_in_ref, convw_ref, convb_ref,
                  wg_ref, ba_ref, bx_ref, lam_ref, lrug_ref, retg_ref, w_out_ref,
                  decay_ref, xi_ref, zeta_ref, out_ref,
                  xl_buf, a_buf, h_buf, h_carry, q_buf, k_buf, v_buf, o_buf, state,
                  *, ts, g_chunk):
    s = pl.program_id(1)

    @pl.when(s == 0)
    def _():
        xl_buf[0:SUBLANES, :] = jnp.zeros((SUBLANES, D_LRU), F32)
        h_carry[...] = jnp.zeros_like(h_carry)
        state[...] = jnp.zeros_like(state)

    x = x_ref[...]
    ub = _rms(x, g1_ref[...]).astype(BF16)

    xl = _wdot(ub, w_in_ref, c0=0, c1=D_LRU)
    xl_buf[SUBLANES:SUBLANES + ts, :] = xl
    cw = convw_ref[...]
    xc = (convb_ref[...]
          + cw[3:4, :] * xl
          + cw[2:3, :] * xl_buf[SUBLANES - 1:SUBLANES - 1 + ts, :]
          + cw[1:2, :] * xl_buf[SUBLANES - 2:SUBLANES - 2 + ts, :]
          + cw[0:1, :] * xl_buf[SUBLANES - 3:SUBLANES - 3 + ts, :])
    xl_buf[0:SUBLANES, :] = xl_buf[ts:ts + SUBLANES, :]
    xcb = xc.astype(BF16)

    q0 = 2 * D_LRU
    proj_cols = {"q": q0, "k": q0 + D_RET, "v": q0 + 2 * D_RET, "g_ret": q0 + 3 * D_RET, "g_lru": D_LRU}

    def proj(name):
        return _wdot(ub, w_in_ref, c0=proj_cols[name], c1=proj_cols[name] + D_RET)

    neg_c_sp = -LRU_C * _softplus(-lam_ref[...])
    for g in range(D_LRU // GATE_GROUP):
        cs = slice(g * GATE_GROUP, (g + 1) * GATE_GROUP)
        gates = _wdot(xcb[:, cs], wg_ref, lead=g)
        r = jax.nn.sigmoid(gates[:, :GATE_GROUP] + ba_ref[:, cs])
        i = jax.nn.sigmoid(gates[:, GATE_GROUP:] + bx_ref[:, cs])
        log_a = neg_c_sp[:, cs] * r
        a = jnp.exp(log_a)
        t = jnp.tanh(log_a)
        mult = jnp.sqrt(-2.0 * t / (1.0 - t))
        a_buf[:, cs] = a
        h_buf[:, cs] = mult * (i * xc[:, cs])

    row = lax.broadcasted_iota(jnp.int32, (SUBLANES, D_LRU), 0)

    def scan_block(j, hprev):
        r0 = pl.multiple_of(j * SUBLANES, SUBLANES)
        a_blk = a_buf[pl.ds(r0, SUBLANES), :]
        b_blk = h_buf[pl.ds(r0, SUBLANES), :]
        for k in (1, 2, 4):
            keep = row >= k
            a_sh = pltpu.roll(a_blk, k, axis=0)
            b_sh = pltpu.roll(b_blk, k, axis=0)
            b_blk = jnp.where(keep, a_blk * b_sh + b_blk, b_blk)
            a_blk = jnp.where(keep, a_blk * a_sh, a_blk)
        h = a_blk * hprev + b_blk
        h_buf[pl.ds(r0, SUBLANES), :] = h
        return jnp.broadcast_to(h[SUBLANES - 1:SUBLANES, :], (SUBLANES, D_LRU))

    h_carry[...] = lax.fori_loop(0, ts // SUBLANES, scan_block, h_carry[...], unroll=True)
    g_lru = proj("g_lru")
    y_lru = _rms(h_buf[...] * _gelu_tanh(g_lru), lrug_ref[...])

    q = proj("q")
    k = proj("k")
    v_buf[...] = proj("v").astype(BF16)
    cosf = cos_ref[...]
    sinf = sin_ref[...]
    half = RET_HEAD_DIM // 2
    for h in range(RET_HEADS):
        hs = slice(h * RET_HEAD_DIM, (h + 1) * RET_HEAD_DIM)
        qh = q[:, hs]
        kh = k[:, hs]
        q_buf[:, hs] = qh * cosf + pltpu.roll(qh, half, axis=1) * sinf
        k_buf[:, hs] = kh * cosf + pltpu.roll(kh, half, axis=1) * sinf

    for c in range(ts // RET_BLOCK):
        rs = slice(c * RET_BLOCK, (c + 1) * RET_BLOCK)
        qc = q_buf[rs, :]
        kc = k_buf[rs, :]
        qx = (qc * xi_ref[...]).astype(BF16)
        kz = (kc * zeta_ref[...]).astype(BF16)
        qcb = qc.astype(BF16)
        kcb = kc.astype(BF16)
        vc = v_buf[rs, :]
        for h in range(RET_HEADS):
            hs = slice(h * RET_HEAD_DIM, (h + 1) * RET_HEAD_DIM)
            scores = lax.dot_general(qcb[:, hs], kcb[:, hs], (((1,), (1,)), ((), ())),
                                     preferred_element_type=F32)
            p = (scores * decay_ref[h]).astype(BF16)
            st = state[h]
            lhs = jnp.concatenate([p, qx[:, hs]], axis=1)
            rhs = jnp.concatenate([vc[:, hs], st.astype(BF16)], axis=0)
            o_buf[rs, hs] = _dot(lhs, rhs)
            kv = lax.dot_general(kz[:, hs], vc[:, hs], (((0,), (0,)), ((), ())),
                                 preferred_element_type=F32)
            state[h] = st * g_chunk[h] + kv

    g_ret = proj("g_ret")
    gate = g_ret * jax.nn.sigmoid(g_ret)
    y_parts = [y_lru.astype(BF16)]
    for h in range(RET_HEADS):
        hs = slice(h * RET_HEAD_DIM, (h + 1) * RET_HEAD_DIM)
        o = o_buf[:, hs]
        mu = jnp.mean(o, axis=-1, keepdims=True)
        d = o - mu
        var = jnp.mean(d * d, axis=-1, keepdims=True)
        y = d * lax.rsqrt(var + NORM_EPS) * retg_ref[:, hs]
        y_parts.append((y * gate[:, hs]).astype(BF16))
    mixed = jnp.concatenate(y_parts, axis=1)
    out_ref[...] = x + _wdot(mixed, w_out_ref)


def _gelu_times(x, v):
    c = math.sqrt(2.0 / math.pi)
    t = jnp.tanh(x * (c + (0.044715 * c) * (x * x)))
    return (0.5 * x * v) * (1.0 + t)


def _ffn_kernel(x_ref, g2_ref, w_up_ref, convw_ref, convb_ref, w_down_ref, gf_ref, out_ref,
                o_buf, *up_bufs, ts):
    s = pl.program_id(1)

    @pl.when(s == 0)
    def _():
        for buf in up_bufs:
            for i in range(buf.shape[0]):
                buf[i, 0:SUBLANES, :] = jnp.zeros((SUBLANES, LANES), F32)

    half = ts // 2
    n_chunks = D_FF // FF_CHUNK
    n_slab = FF_CHUNK // LANES
    x = x_ref[...]
    ub = _rms(x, g2_ref[...]).astype(BF16)
    acc = None

    def up_dot(j):
        return (_wdot(ub, w_up_ref, c0=j * FF_CHUNK, c1=(j + 1) * FF_CHUNK),
                _wdot(ub, w_up_ref, c0=D_FF + j * FF_CHUNK, c1=D_FF + (j + 1) * FF_CHUNK))

    ups = [up_dot(j) for j in range(n_chunks)]
    for j in range(n_chunks):
        conv = []
        for k in range(2 * n_slab):
            br, kk = divmod(k, n_slab)
            c0 = br * D_FF + j * FF_CHUNK + kk * LANES
            cs = slice(c0, c0 + LANES)
            buf = up_bufs[j]
            buf[k, SUBLANES:SUBLANES + ts, :] = ups[j][br][:, kk * LANES:(kk + 1) * LANES]
            x_e = buf[k, pl.ds(SUBLANES, half, stride=2), :]
            x_o = buf[k, pl.ds(SUBLANES + 1, half, stride=2), :]
            x_om = buf[k, pl.ds(SUBLANES - 1, half, stride=2), :]
            x_em = buf[k, pl.ds(SUBLANES - 2, half, stride=2), :]
            buf[k, 0:SUBLANES, :] = buf[k, ts:ts + SUBLANES, :]
            w0 = jnp.broadcast_to(convw_ref[0:1, cs], (half, LANES))
            w1 = jnp.broadcast_to(convw_ref[1:2, cs], (half, LANES))
            w2 = jnp.broadcast_to(convw_ref[2:3, cs], (half, LANES))
            b = jnp.broadcast_to(convb_ref[:, cs], (half, LANES))
            y_e = b + w2 * x_e + w1 * x_om + w0 * x_em
            y_o = b + w2 * x_o + w1 * x_e + w0 * x_om
            conv.append((y_e, y_o))
        gated = jnp.concatenate(
            [jnp.concatenate([_gelu_times(conv[k][p], conv[k + n_slab][p]) for p in (0, 1)], axis=0)
             for k in range(n_slab)], axis=1).astype(BF16)
        d = _wdot(gated, w_down_ref, r0=j * FF_CHUNK, r1=(j + 1) * FF_CHUNK)
        acc = d if acc is None else acc + d
    for k in range(D_MODEL // LANES):
        o_buf[k, pl.ds(0, half, stride=2), :] = acc[0:half, k * LANES:(k + 1) * LANES]
        o_buf[k, pl.ds(1, half, stride=2), :] = acc[half:ts, k * LANES:(k + 1) * LANES]
    y = jnp.concatenate([o_buf[k] for k in range(D_MODEL // LANES)], axis=1)
    out_ref[...] = _rms(x + y, gf_ref[...])


def _const_spec(shape):
    n = len(shape)
    return pl.BlockSpec(shape, lambda b, s: (0,) * n, pipeline_mode=pl.Buffered(1))


def _block_diag(w):
    heads, d, _ = w.shape
    eye = jnp.eye(heads, dtype=w.dtype)
    return jnp.einsum("hij,hg->higj", w, eye).reshape(heads * d, heads * d)


def _retention_tables():
    c = RET_BLOCK
    log_g = np.log1p(-np.exp2(-5.0 - np.arange(RET_HEADS, dtype=np.float64)))
    idx = np.arange(c, dtype=np.float64)
    diff = idx[:, None] - idx[None, :]
    scale = RET_HEAD_DIM ** -0.5
    decay = np.where(diff[None] >= 0, np.exp(np.maximum(diff, 0.0)[None] * log_g[:, None, None]), 0.0)
    zeta = np.exp((c - 1 - idx)[None, :] * log_g[:, None])
    xi = np.exp((idx + 1.0)[None, :] * log_g[:, None])
    g_chunk = tuple(float(np.float32(np.exp(c * lg))) for lg in log_g)
    decay = (decay * scale).astype(np.float32)
    zeta_tab = np.repeat((zeta * scale).T, RET_HEAD_DIM, axis=1).astype(np.float32)
    xi_tab = np.repeat(xi.T, RET_HEAD_DIM, axis=1).astype(np.float32)
    return decay, xi_tab, zeta_tab, g_chunk


def _rotary_tables(slen):
    pos = jnp.arange(slen, dtype=F32)
    inv_freq = ROPE_BASE ** (-jnp.arange(0, RET_HEAD_DIM, 2, dtype=F32) / RET_HEAD_DIM)
    ang = pos[:, None] * inv_freq[None, :]
    cos = jnp.cos(ang)
    sin = jnp.sin(ang)
    return jnp.concatenate([cos, cos], axis=1), jnp.concatenate([-sin, sin], axis=1)


def kernel(x, norm1_gain, w_in, lru_conv_w, lru_conv_b, lru_gate_a_w, lru_gate_a_b,
           lru_gate_x_w, lru_gate_x_b, lru_lambda, lru_norm_gain, ret_norm_gain, w_out,
           norm2_gain, ffn_up_w, ffn_conv_w, ffn_conv_b, ffn_down_w, final_norm_gain):
    bsz, slen, d_model = x.shape
    depth = w_in.shape[0]
    assert d_model == D_MODEL and slen % SEQ_TILE == 0 and SEQ_TILE % RET_BLOCK == 0
    ts = SEQ_TILE
    grid = (bsz, slen // ts)
    cosf, sinf = _rotary_tables(slen)
    decay, xi_tab, zeta_tab, g_chunk = _retention_tables()
    params = pltpu.CompilerParams(dimension_semantics=("arbitrary", "arbitrary"),
                                  vmem_limit_bytes=VMEM_LIMIT_BYTES)
    tile_spec = pl.BlockSpec((None, ts, D_MODEL), lambda b, s: (b, s, 0))
    rot_spec = pl.BlockSpec((ts, RET_HEAD_DIM), lambda b, s: (s, 0))
    row = lambda a: a.reshape(1, -1).astype(F32)

    h = x
    for l in range(depth):
        wa = _block_diag(lru_gate_a_w[l])
        wx = _block_diag(lru_gate_x_w[l])
        n_g = D_LRU // GATE_GROUP
        wg = jnp.stack([
            jnp.concatenate([wa[g * GATE_GROUP:(g + 1) * GATE_GROUP, g * GATE_GROUP:(g + 1) * GATE_GROUP],
                             wx[g * GATE_GROUP:(g + 1) * GATE_GROUP, g * GATE_GROUP:(g + 1) * GATE_GROUP]],
                            axis=1)
            for g in range(n_g)])
        wg = jnp.stack([_pack_rows(wg[g]) for g in range(n_g)])

        mixer = pl.pallas_call(
            functools.partial(_mixer_kernel, ts=ts, g_chunk=g_chunk),
            name="token_mixer",
            grid=grid,
            in_specs=[
                tile_spec, rot_spec, rot_spec,
                _const_spec((1, D_MODEL)),
                _const_spec((D_MODEL // 2, D_IN)),
                _const_spec((LRU_CONV, D_LRU)),
                _const_spec((1, D_LRU)),
                _const_spec((n_g, GATE_GROUP // 2, 2 * GATE_GROUP)),
                _const_spec((1, D_LRU)), _const_spec((1, D_LRU)), _const_spec((1, D_LRU)),
                _const_spec((1, D_LRU)), _const_spec((1, D_RET)),
                _const_spec((D_MODEL // 2, D_MODEL)),
                _const_spec((RET_HEADS, RET_BLOCK, RET_BLOCK)),
                _const_spec((RET_BLOCK, D_RET)), _const_spec((RET_BLOCK, D_RET)),
            ],
            out_specs=tile_spec,
            out_shape=jax.ShapeDtypeStruct((bsz, slen, D_MODEL), F32),
            scratch_shapes=[
                pltpu.VMEM((ts + SUBLANES, D_LRU), F32),
                pltpu.VMEM((ts, D_LRU), F32),
                pltpu.VMEM((ts, D_LRU), F32),
                pltpu.VMEM((SUBLANES, D_LRU), F32),
                pltpu.VMEM((ts, D_RET), F32),
                pltpu.VMEM((ts, D_RET), F32),
                pltpu.VMEM((ts, D_RET), BF16),
                pltpu.VMEM((ts, D_RET), F32),
                pltpu.VMEM((RET_HEADS, RET_HEAD_DIM, RET_HEAD_DIM), F32),
            ],
            compiler_params=params,
        )
        h = mixer(h, cosf, sinf, row(norm1_gain[l]), _pack_rows(w_in[l]),
                  lru_conv_w[l].astype(F32), row(lru_conv_b[l]), wg,
                  row(lru_gate_a_b[l]), row(lru_gate_x_b[l]), row(lru_lambda[l]),
                  row(lru_norm_gain[l]), row(ret_norm_gain[l]), _pack_rows(w_out[l]),
                  jnp.asarray(decay), jnp.asarray(xi_tab), jnp.asarray(zeta_tab))

        last = l == depth - 1
        gf = row(final_norm_gain) if last else None
        assert last, "final norm is fused into the last layer's channel mixer"
        n_chunks = D_FF // FF_CHUNK
        n_slab2 = 2 * FF_CHUNK // LANES

        ffn = pl.pallas_call(
            functools.partial(_ffn_kernel, ts=ts),
            name="channel_mixer",
            grid=grid,
            in_specs=[
                tile_spec,
                _const_spec((1, D_MODEL)),
                _const_spec((D_MODEL // 2, 2 * D_FF)),
                _const_spec((FFN_CONV, 2 * D_FF)),
                _const_spec((1, 2 * D_FF)),
                _const_spec((D_FF // 2, D_MODEL)),
                _const_spec((1, D_MODEL)),
            ],
            out_specs=tile_spec,
            out_shape=jax.ShapeDtypeStruct((bsz, slen, D_MODEL), F32),
            scratch_shapes=[
                pltpu.VMEM((D_MODEL // LANES, ts, LANES), F32),
            ] + [pltpu.VMEM((n_slab2, ts + SUBLANES, LANES), F32)] * n_chunks,
            compiler_params=params,
        )
        h = ffn(h, row(norm2_gain[l]), _pack_rows(ffn_up_w[l]),
                ffn_conv_w[l].astype(F32), row(ffn_conv_b[l]),
                _pack_rows(ffn_down_w[l]), gf)
    return h
```

```python
import functools
import math

import numpy as np
import jax
import jax.numpy as jnp
from jax import lax
from jax.experimental import pallas as pl
from jax.experimental.pallas import tpu as pltpu

D_MODEL = 1024
D_LRU = 512
LRU_HEADS = 8
LRU_HEAD_DIM = D_LRU // LRU_HEADS
LRU_CONV = 4
LRU_C = 8.0
D_RET = 512
RET_HEADS = 4
RET_HEAD_DIM = D_RET // RET_HEADS
RET_CHUNK = 128
ROPE_BASE = 10000.0
D_IN = 2 * D_LRU + 4 * D_RET
D_FF = 3 * D_MODEL
FFN_CONV = 3
NORM_EPS = 1e-6

SUBLANES = 8
LANES = 128
MXU_DIM = 256
VMEM_LIMIT_BYTES = 56 * 1024 * 1024

SEQ_TILE = 512
RET_BLOCK = 256
FF_CHUNK = 512
GATE_GROUP = MXU_DIM
PACK_ROWS = 256

BF16 = jnp.bfloat16
F32 = jnp.float32


def _rms(x, gain):
    ms = jnp.mean(x * x, axis=-1, keepdims=True)
    return x * lax.rsqrt(ms + NORM_EPS) * gain


def _gelu_tanh(x):
    c = math.sqrt(2.0 / math.pi)
    cdf = 0.5 * (1.0 + jnp.tanh(c * (x + 0.044715 * (x * x * x))))
    return x * cdf


def _softplus(x):
    return jnp.maximum(x, 0.0) + jnp.log1p(jnp.exp(-jnp.abs(x)))


def _dot(a, b):
    return jnp.dot(a, b, preferred_element_type=F32)


def _pack_rows(w):
    k, n = w.shape
    rb = min(k, PACK_ROWS)
    assert k % rb == 0 and rb % (2 * SUBLANES) == 0
    return pl.pallas_call(
        _pack_kernel,
        name="pack_weight",
        grid=(k // rb,),
        in_specs=[pl.BlockSpec((rb, n), lambda i: (i, 0))],
        out_specs=pl.BlockSpec((rb // 2, n), lambda i: (i, 0)),
        out_shape=jax.ShapeDtypeStruct((k // 2, n), jnp.uint32),
        compiler_params=pltpu.CompilerParams(dimension_semantics=("arbitrary",),
                                             vmem_limit_bytes=VMEM_LIMIT_BYTES),
    )(w.astype(F32))


def _pack_kernel(w_ref, o_ref):
    o_ref[...] = pltpu.bitcast(w_ref[...].astype(BF16), jnp.uint32)


def _wdot(a, w_ref, r0=None, r1=None, c0=None, c1=None, lead=None):
    rs = slice(None) if r0 is None else slice(r0 // 2, r1 // 2)
    cs = slice(None) if c0 is None else slice(c0, c1)
    packed = w_ref[rs, cs] if lead is None else w_ref[lead, rs, cs]
    return _dot(a, pltpu.bitcast(packed, BF16))


def _mixer_kernel(x_ref, cos_ref, sin_ref, g1_ref, w_in_ref, convw_ref, convb_ref,
                  wg_ref, ba_ref, bx_ref, lam_ref, lrug_ref, retg_ref, w_out_ref,
                  decay_ref, xi_ref, zeta_ref, out_ref,
                  xl_buf, a_buf, h_buf, h_carry, q_buf, k_buf, v_buf, o_buf, state,
                  *, ts, g_chunk):
    s = pl.program_id(1)

    @pl.when(s == 0)
    def _():
        xl_buf[0:SUBLANES, :] = jnp.zeros((SUBLANES, D_LRU), F32)
        h_carry[...] = jnp.zeros_like(h_carry)
        state[...] = jnp.zeros_like(state)

    x = x_ref[...]
    ub = _rms(x, g1_ref[...]).astype(BF16)

    xl = _wdot(ub, w_in_ref, c0=0, c1=D_LRU)
    xl_buf[SUBLANES:SUBLANES + ts, :] = xl
    cw = convw_ref[...]
    xc = (convb_ref[...]
          + cw[3:4, :] * xl
          + cw[2:3, :] * xl_buf[SUBLANES - 1:SUBLANES - 1 + ts, :]
          + cw[1:2, :] * xl_buf[SUBLANES - 2:SUBLANES - 2 + ts, :]
          + cw[0:1, :] * xl_buf[SUBLANES - 3:SUBLANES - 3 + ts, :])
    xl_buf[0:SUBLANES, :] = xl_buf[ts:ts + SUBLANES, :]
    xcb = xc.astype(BF16)

    q0 = 2 * D_LRU
    proj_cols = {"q": q0, "k": q0 + D_RET, "v": q0 + 2 * D_RET, "g_ret": q0 + 3 * D_RET, "g_lru": D_LRU}

    def proj(name):
        return _wdot(ub, w_in_ref, c0=proj_cols[name], c1=proj_cols[name] + D_RET)

    neg_c_sp = -LRU_C * _softplus(-lam_ref[...])
    for g in range(D_LRU // GATE_GROUP):
        cs = slice(g * GATE_GROUP, (g + 1) * GATE_GROUP)
        gates = _wdot(xcb[:, cs], wg_ref, lead=g)
        r = jax.nn.sigmoid(gates[:, :GATE_GROUP] + ba_ref[:, cs])
        i = jax.nn.sigmoid(gates[:, GATE_GROUP:] + bx_ref[:, cs])
        log_a = neg_c_sp[:, cs] * r
        a = jnp.exp(log_a)
        t = jnp.tanh(log_a)
        mult = jnp.sqrt(-2.0 * t / (1.0 - t))
        a_buf[:, cs] = a
        h_buf[:, cs] = mult * (i * xc[:, cs])

    row = lax.broadcasted_iota(jnp.int32, (SUBLANES, D_LRU), 0)

    def scan_block(j, hprev):
        r0 = pl.multiple_of(j * SUBLANES, SUBLANES)
        a_blk = a_buf[pl.ds(r0, SUBLANES), :]
        b_blk = h_buf[pl.ds(r0, SUBLANES), :]
        for k in (1, 2, 4):
            keep = row >= k
            a_sh = pltpu.roll(a_blk, k, axis=0)
            b_sh = pltpu.roll(b_blk, k, axis=0)
            b_blk = jnp.where(keep, a_blk * b_sh + b_blk, b_blk)
            a_blk = jnp.where(keep, a_blk * a_sh, a_blk)
        h = a_blk * hprev + b_blk
        h_buf[pl.ds(r0, SUBLANES), :] = h
        return jnp.broadcast_to(h[SUBLANES - 1:SUBLANES, :], (SUBLANES, D_LRU))

    h_carry[...] = lax.fori_loop(0, ts // SUBLANES, scan_block, h_carry[...], unroll=True)
    g_lru = proj("g_lru")
    y_lru = _rms(h_buf[...] * _gelu_tanh(g_lru), lrug_ref[...])

    q = proj("q")
    k = proj("k")
    v_buf[...] = proj("v").astype(BF16)
    cosf = cos_ref[...]
    sinf = sin_ref[...]
    half = RET_HEAD_DIM // 2
    for h in range(RET_HEADS):
        hs = slice(h * RET_HEAD_DIM, (h + 1) * RET_HEAD_DIM)
        qh = q[:, hs]
        kh = k[:, hs]
        q_buf[:, hs] = qh * cosf + pltpu.roll(qh, half, axis=1) * sinf
        k_buf[:, hs] = kh * cosf + pltpu.roll(kh, half, axis=1) * sinf

    for c in range(ts // RET_BLOCK):
        rs = slice(c * RET_BLOCK, (c + 1) * RET_BLOCK)
        qc = q_buf[rs, :]
        kc = k_buf[rs, :]
        qx = (qc * xi_ref[...]).astype(BF16)
        kz = (kc * zeta_ref[...]).astype(BF16)
        qcb = qc.astype(BF16)
        kcb = kc.astype(BF16)
        vc = v_buf[rs, :]
        for h in range(RET_HEADS):
            hs = slice(h * RET_HEAD_DIM, (h + 1) * RET_HEAD_DIM)
            scores = lax.dot_general(qcb[:, hs], kcb[:, hs], (((1,), (1,)), ((), ())),
                                     preferred_element_type=F32)
            p = (scores * decay_ref[h]).astype(BF16)
            st = state[h]
            lhs = jnp.concatenate([p, qx[:, hs]], axis=1)
            rhs = jnp.concatenate([vc[:, hs], st.astype(BF16)], axis=0)
            o_buf[rs, hs] = _dot(lhs, rhs)
            kv = lax.dot_general(kz[:, hs], vc[:, hs], (((0,), (0,)), ((), ())),
                                 preferred_element_type=F32)
            state[h] = st * g_chunk[h] + kv

    g_ret = proj("g_ret")
    gate = g_ret * jax.nn.sigmoid(g_ret)
    y_parts = [y_lru.astype(BF16)]
    for h in range(RET_HEADS):
        hs = slice(h * RET_HEAD_DIM, (h + 1) * RET_HEAD_DIM)
        o = o_buf[:, hs]
        mu = jnp.mean(o, axis=-1, keepdims=True)
        d = o - mu
        var = jnp.mean(d * d, axis=-1, keepdims=True)
        y = d * lax.rsqrt(var + NORM_EPS) * retg_ref[:, hs]
        y_parts.append((y * gate[:, hs]).astype(BF16))
    mixed = jnp.concatenate(y_parts, axis=1)
    out_ref[...] = x + _wdot(mixed, w_out_ref)


def _gelu_times(x, v):
    c = math.sqrt(2.0 / math.pi)
    t = jnp.tanh(x * (c + (0.044715 * c) * (x * x)))
    return (0.5 * x * v) * (1.0 + t)


def _ffn_kernel(x_ref, g2_ref, w_up_ref, convw_ref, convb_ref, w_down_ref, gf_ref, out_ref,
                o_buf, *up_bufs, ts):
    s = pl.program_id(1)

    @pl.when(s == 0)
    def _():
        for buf in up_bufs:
            for i in range(buf.shape[0]):
                buf[i, 0:SUBLANES, :] = jnp.zeros((SUBLANES, LANES), F32)

    half = ts // 2
    n_chunks = D_FF // FF_CHUNK
    n_slab = FF_CHUNK // LANES
    x = x_ref[...]
    ub = _rms(x, g2_ref[...]).astype(BF16)
    acc = None

    def up_dot(j):
        return (_wdot(ub, w_up_ref, c0=j * FF_CHUNK, c1=(j + 1) * FF_CHUNK),
                _wdot(ub, w_up_ref, c0=D_FF + j * FF_CHUNK, c1=D_FF + (j + 1) * FF_CHUNK))

    ups = [up_dot(j) for j in range(n_chunks)]
    for j in range(n_chunks):
        conv = []
        for k in range(2 * n_slab):
            br, kk = divmod(k, n_slab)
            c0 = br * D_FF + j * FF_CHUNK + kk * LANES
            cs = slice(c0, c0 + LANES)
            buf = up_bufs[j]
            buf[k, SUBLANES:SUBLANES + ts, :] = ups[j][br][:, kk * LANES:(kk + 1) * LANES]
            x_e = buf[k, pl.ds(SUBLANES, half, stride=2), :]
            x_o = buf[k, pl.ds(SUBLANES + 1, half, stride=2), :]
            x_om = buf[k, pl.ds(SUBLANES - 1, half, stride=2), :]
            x_em = buf[k, pl.ds(SUBLANES - 2, half, stride=2), :]
            buf[k, 0:SUBLANES, :] = buf[k, ts:ts + SUBLANES, :]
            w0 = jnp.broadcast_to(convw_ref[0:1, cs], (half, LANES))
            w1 = jnp.broadcast_to(convw_ref[1:2, cs], (half, LANES))
            w2 = jnp.broadcast_to(convw_ref[2:3, cs], (half, LANES))
            b = jnp.broadcast_to(convb_ref[:, cs], (half, LANES))
            y_e = b + w2 * x_e + w1 * x_om + w0 * x_em
            y_o = b + w2 * x_o + w1 * x_e + w0 * x_om
            conv.append((y_e, y_o))
        gated = jnp.concatenate(
            [jnp.concatenate([_gelu_times(conv[k][p], conv[k + n_slab][p]) for p in (0, 1)], axis=0)
             for k in range(n_slab)], axis=1).astype(BF16)
        d = _wdot(gated, w_down_ref, r0=j * FF_CHUNK, r1=(j + 1) * FF_CHUNK)
        acc = d if acc is None else acc + d
    for k in range(D_MODEL // LANES):
        o_buf[k, pl.ds(0, half, stride=2), :] = acc[0:half, k * LANES:(k + 1) * LANES]
        o_buf[k, pl.ds(1, half, stride=2), :] = acc[half:ts, k * LANES:(k + 1) * LANES]
    y = jnp.concatenate([o_buf[k] for k in range(D_MODEL // LANES)], axis=1)
    out_ref[...] = _rms(x + y, gf_ref[...])


def _const_spec(shape):
    n = len(shape)
    return pl.BlockSpec(shape, lambda b, s: (0,) * n, pipeline_mode=pl.Buffered(1))


def _block_diag(w):
    heads, d, _ = w.shape
    eye = jnp.eye(heads, dtype=w.dtype)
    return jnp.einsum("hij,hg->higj", w, eye).reshape(heads * d, heads * d)


def _retention_tables():
    c = RET_BLOCK
    log_g = np.log1p(-np.exp2(-5.0 - np.arange(RET_HEADS, dtype=np.float64)))
    idx = np.arange(c, dtype=np.float64)
    diff = idx[:, None] - idx[None, :]
    scale = RET_HEAD_DIM ** -0.5
    decay = np.where(diff[None] >= 0, np.exp(np.maximum(diff, 0.0)[None] * log_g[:, None, None]), 0.0)
    zeta = np.exp((c - 1 - idx)[None, :] * log_g[:, None])
    xi = np.exp((idx + 1.0)[None, :] * log_g[:, None])
    g_chunk = tuple(float(np.float32(np.exp(c * lg))) for lg in log_g)
    decay = (decay * scale).astype(np.float32)
    zeta_tab = np.repeat((zeta * scale).T, RET_HEAD_DIM, axis=1).astype(np.float32)
    xi_tab = np.repeat(xi.T, RET_HEAD_DIM, axis=1).astype(np.float32)
    return decay, xi_tab, zeta_tab, g_chunk


def _rotary_tables(slen):
    pos = np.arange(slen, dtype=np.float32)
    expo = -np.arange(0, RET_HEAD_DIM, 2, dtype=np.float32) / np.float32(RET_HEAD_DIM)
    inv_freq = np.power(np.float32(ROPE_BASE), expo).astype(np.float32)
    ang = (pos[:, None] * inv_freq[None, :]).astype(np.float32).astype(np.float64)
    cos = np.cos(ang).astype(np.float32)
    sin = np.sin(ang).astype(np.float32)
    return np.concatenate([cos, cos], axis=1), np.concatenate([-sin, sin], axis=1)


def kernel(x, norm1_gain, w_in, lru_conv_w, lru_conv_b, lru_gate_a_w, lru_gate_a_b,
           lru_gate_x_w, lru_gate_x_b, lru_lambda, lru_norm_gain, ret_norm_gain, w_out,
           norm2_gain, ffn_up_w, ffn_conv_w, ffn_conv_b, ffn_down_w, final_norm_gain):
    bsz, slen, d_model = x.shape
    depth = w_in.shape[0]
    assert d_model == D_MODEL and slen % SEQ_TILE == 0 and SEQ_TILE % RET_BLOCK == 0
    ts = SEQ_TILE
    grid = (bsz, slen // ts)
    cosf, sinf = _rotary_tables(slen)
    decay, xi_tab, zeta_tab, g_chunk = _retention_tables()
    params = pltpu.CompilerParams(dimension_semantics=("arbitrary", "arbitrary"),
                                  vmem_limit_bytes=VMEM_LIMIT_BYTES)
    tile_spec = pl.BlockSpec((None, ts, D_MODEL), lambda b, s: (b, s, 0))
    rot_spec = pl.BlockSpec((ts, RET_HEAD_DIM), lambda b, s: (s, 0))
    row = lambda a: a.reshape(1, -1).astype(F32)

    h = x
    for l in range(depth):
        wa = _block_diag(lru_gate_a_w[l])
        wx = _block_diag(lru_gate_x_w[l])
        n_g = D_LRU // GATE_GROUP
        wg = jnp.stack([
            jnp.concatenate([wa[g * GATE_GROUP:(g + 1) * GATE_GROUP, g * GATE_GROUP:(g + 1) * GATE_GROUP],
                             wx[g * GATE_GROUP:(g + 1) * GATE_GROUP, g * GATE_GROUP:(g + 1) * GATE_GROUP]],
                            axis=1)
            for g in range(n_g)])
        wg = _pack_rows(wg.reshape(n_g * GATE_GROUP, 2 * GATE_GROUP))
        wg = wg.reshape(n_g, GATE_GROUP // 2, 2 * GATE_GROUP)

        mixer = pl.pallas_call(
            functools.partial(_mixer_kernel, ts=ts, g_chunk=g_chunk),
            name="token_mixer",
            grid=grid,
            in_specs=[
                tile_spec, rot_spec, rot_spec,
                _const_spec((1, D_MODEL)),
                _const_spec((D_MODEL // 2, D_IN)),
                _const_spec((LRU_CONV, D_LRU)),
                _const_spec((1, D_LRU)),
                _const_spec((n_g, GATE_GROUP // 2, 2 * GATE_GROUP)),
                _const_spec((1, D_LRU)), _const_spec((1, D_LRU)), _const_spec((1, D_LRU)),
                _const_spec((1, D_LRU)), _const_spec((1, D_RET)),
                _const_spec((D_MODEL // 2, D_MODEL)),
                _const_spec((RET_HEADS, RET_BLOCK, RET_BLOCK)),
                _const_spec((RET_BLOCK, D_RET)), _const_spec((RET_BLOCK, D_RET)),
            ],
            out_specs=tile_spec,
            out_shape=jax.ShapeDtypeStruct((bsz, slen, D_MODEL), F32),
            scratch_shapes=[
                pltpu.VMEM((ts + SUBLANES, D_LRU), F32),
                pltpu.VMEM((ts, D_LRU), F32),
                pltpu.VMEM((ts, D_LRU), F32),
                pltpu.VMEM((SUBLANES, D_LRU), F32),
                pltpu.VMEM((ts, D_RET), F32),
                pltpu.VMEM((ts, D_RET), F32),
                pltpu.VMEM((ts, D_RET), BF16),
                pltpu.VMEM((ts, D_RET), F32),
                pltpu.VMEM((RET_HEADS, RET_HEAD_DIM, RET_HEAD_DIM), F32),
            ],
            compiler_params=params,
        )
        h = mixer(h, cosf, sinf, row(norm1_gain[l]), _pack_rows(w_in[l]),
                  lru_conv_w[l].astype(F32), row(lru_conv_b[l]), wg,
                  row(lru_gate_a_b[l]), row(lru_gate_x_b[l]), row(lru_lambda[l]),
                  row(lru_norm_gain[l]), row(ret_norm_gain[l]), _pack_rows(w_out[l]),
                  jnp.asarray(decay), jnp.asarray(xi_tab), jnp.asarray(zeta_tab))

        last = l == depth - 1
        gf = row(final_norm_gain) if last else None
        assert last, "final norm is fused into the last layer's channel mixer"
        n_chunks = D_FF // FF_CHUNK
        n_slab2 = 2 * FF_CHUNK // LANES

        ffn = pl.pallas_call(
            functools.partial(_ffn_kernel, ts=ts),
            name="channel_mixer",
            grid=grid,
            in_specs=[
                tile_spec,
                _const_spec((1, D_MODEL)),
                _const_spec((D_MODEL // 2, 2 * D_FF)),
                _const_spec((FFN_CONV, 2 * D_FF)),
                _const_spec((1, 2 * D_FF)),
                _const_spec((D_FF // 2, D_MODEL)),
                _const_spec((1, D_MODEL)),
            ],
            out_specs=tile_spec,
            out_shape=jax.ShapeDtypeStruct((bsz, slen, D_MODEL), F32),
            scratch_shapes=[
                pltpu.VMEM((D_MODEL // LANES, ts, LANES), F32),
            ] + [pltpu.VMEM((n_slab2, ts + SUBLANES, LANES), F32)] * n_chunks,
            compiler_params=params,
        )
        h = ffn(h, row(norm2_gain[l]), _pack_rows(ffn_up_w[l]),
                ffn_conv_w[l].astype(F32), row(ffn_conv_b[l]),
                _pack_rows(ffn_down_w[l]), gf)
    return h
```

```python
import functools
import math

import numpy as np
import jax
import jax.numpy as jnp
from jax import lax
from jax.experimental import pallas as pl
from jax.experimental.pallas import tpu as pltpu

D_MODEL = 1024
D_LRU = 512
LRU_HEADS = 8
LRU_HEAD_DIM = D_LRU // LRU_HEADS
LRU_CONV = 4
LRU_C = 8.0
D_RET = 512
RET_HEADS = 4
RET_HEAD_DIM = D_RET // RET_HEADS
RET_CHUNK = 128
ROPE_BASE = 10000.0
D_IN = 2 * D_LRU + 4 * D_RET
D_FF = 3 * D_MODEL
FFN_CONV = 3
NORM_EPS = 1e-6

SUBLANES = 8
LANES = 128
MXU_DIM = 256
VMEM_LIMIT_BYTES = 56 * 1024 * 1024

SEQ_TILE = 512
MIX_TILE = 1024
RET_BLOCK = 256
FF_CHUNK = 512
GELU_HALF = 0.5
GATE_GROUP = MXU_DIM
PACK_BLOCK_BYTES = 8 * 1024 * 1024

BF16 = jnp.bfloat16
F32 = jnp.float32


def _rms(x, gain):
    ms = jnp.mean(x * x, axis=-1, keepdims=True)
    return x * lax.rsqrt(ms + NORM_EPS) * gain


def _gelu_tanh(x):
    c = math.sqrt(2.0 / math.pi)
    cdf = 0.5 * (1.0 + jnp.tanh(c * (x + 0.044715 * (x * x * x))))
    return x * cdf


def _softplus(x):
    return jnp.maximum(x, 0.0) + jnp.log1p(jnp.exp(-jnp.abs(x)))


def _dot(a, b):
    return jnp.dot(a, b, preferred_element_type=F32)


def _pack_rows(w, scale=1.0):
    k, n = w.shape
    rb = k
    while rb * n * 4 > PACK_BLOCK_BYTES and rb % (4 * SUBLANES) == 0:
        rb //= 2
    assert k % rb == 0 and rb % (2 * SUBLANES) == 0
    return pl.pallas_call(
        functools.partial(_pack_kernel, scale=scale),
        name="pack_weight",
        grid=(k // rb,),
        in_specs=[pl.BlockSpec((rb, n), lambda i: (i, 0))],
        out_specs=pl.BlockSpec((rb // 2, n), lambda i: (i, 0)),
        out_shape=jax.ShapeDtypeStruct((k // 2, n), jnp.uint32),
        compiler_params=pltpu.CompilerParams(dimension_semantics=("arbitrary",),
                                             vmem_limit_bytes=VMEM_LIMIT_BYTES),
    )(w.astype(F32))


def _pack_kernel(w_ref, o_ref, *, scale):
    w = w_ref[...]
    if scale != 1.0:
        w = w * scale
    o_ref[...] = pltpu.bitcast(w.astype(BF16), jnp.uint32)


def _wdot(a, w_ref, r0=None, r1=None, c0=None, c1=None, lead=None):
    rs = slice(None) if r0 is None else slice(r0 // 2, r1 // 2)
    cs = slice(None) if c0 is None else slice(c0, c1)
    packed = w_ref[rs, cs] if lead is None else w_ref[lead, rs, cs]
    return _dot(a, pltpu.bitcast(packed, BF16))


def _mixer_kernel(x_ref, cos_ref, sin_ref, g1_ref, w_in_ref, convw_ref, convb_ref,
                  wg_ref, ba_ref, bx_ref, lam_ref, lrug_ref, retg_ref, w_out_ref,
                  decay_ref, xi_ref, zeta_ref, out_ref,
                  xl_buf, a_buf, h_buf, h_carry, q_buf, k_buf, v_buf, o_buf, state,
                  *, ts, g_chunk):
    s = pl.program_id(1)

    @pl.when(s == 0)
    def _():
        xl_buf[0:SUBLANES, :] = jnp.zeros((SUBLANES, D_LRU), F32)
        h_carry[...] = jnp.zeros_like(h_carry)
        state[...] = jnp.zeros_like(state)

    x = x_ref[...]
    ubs = [_rms(x[p * (ts // 2):(p + 1) * (ts // 2), :], g1_ref[...]).astype(BF16) for p in (0, 1)]
    ub = jnp.concatenate(ubs, axis=0)

    xl = jnp.concatenate([_wdot(u, w_in_ref, c0=0, c1=D_LRU) for u in ubs], axis=0)
    xl_buf[SUBLANES:SUBLANES + ts, :] = xl
    cw = convw_ref[...]
    xc = (convb_ref[...]
          + cw[3:4, :] * xl
          + cw[2:3, :] * xl_buf[SUBLANES - 1:SUBLANES - 1 + ts, :]
          + cw[1:2, :] * xl_buf[SUBLANES - 2:SUBLANES - 2 + ts, :]
          + cw[0:1, :] * xl_buf[SUBLANES - 3:SUBLANES - 3 + ts, :])
    xl_buf[0:SUBLANES, :] = xl_buf[ts:ts + SUBLANES, :]
    xcb = xc.astype(BF16)

    q0 = 2 * D_LRU
    proj_cols = {"q": q0, "k": q0 + D_RET, "v": q0 + 2 * D_RET, "g_ret": q0 + 3 * D_RET, "g_lru": D_LRU}

    def proj(name):
        return _wdot(ub, w_in_ref, c0=proj_cols[name], c1=proj_cols[name] + D_RET)

    neg_c_sp = -LRU_C * _softplus(-lam_ref[...])
    for g in range(D_LRU // GATE_GROUP):
        cs = slice(g * GATE_GROUP, (g + 1) * GATE_GROUP)
        gates = _wdot(xcb[:, cs], wg_ref, lead=g)
        r = jax.nn.sigmoid(gates[:, :GATE_GROUP] + ba_ref[:, cs])
        i = jax.nn.sigmoid(gates[:, GATE_GROUP:] + bx_ref[:, cs])
        log_a = neg_c_sp[:, cs] * r
        a = jnp.exp(log_a)
        t = jnp.tanh(log_a)
        mult = jnp.sqrt(-2.0 * t / (1.0 - t))
        a_buf[:, cs] = a
        h_buf[:, cs] = mult * (i * xc[:, cs])

    row = lax.broadcasted_iota(jnp.int32, (SUBLANES, D_LRU), 0)

    def scan_block(j, hprev):
        r0 = pl.multiple_of(j * SUBLANES, SUBLANES)
        a_blk = a_buf[pl.ds(r0, SUBLANES), :]
        b_blk = h_buf[pl.ds(r0, SUBLANES), :]
        for k in (1, 2, 4):
            keep = row >= k
            a_sh = pltpu.roll(a_blk, k, axis=0)
            b_sh = pltpu.roll(b_blk, k, axis=0)
            b_blk = jnp.where(keep, a_blk * b_sh + b_blk, b_blk)
            a_blk = jnp.where(keep, a_blk * a_sh, a_blk)
        h = a_blk * hprev + b_blk
        h_buf[pl.ds(r0, SUBLANES), :] = h
        return jnp.broadcast_to(h[SUBLANES - 1:SUBLANES, :], (SUBLANES, D_LRU))

    h_carry[...] = lax.fori_loop(0, ts // SUBLANES, scan_block, h_carry[...], unroll=True)
    g_lru = proj("g_lru")
    y_lru = _rms(h_buf[...] * _gelu_tanh(g_lru), lrug_ref[...])

    q = proj("q")
    k = proj("k")
    v_buf[...] = proj("v").astype(BF16)
    cosf = cos_ref[...]
    sinf = sin_ref[...]
    half = RET_HEAD_DIM // 2
    for h in range(RET_HEADS):
        hs = slice(h * RET_HEAD_DIM, (h + 1) * RET_HEAD_DIM)
        qh = q[:, hs]
        kh = k[:, hs]
        q_buf[:, hs] = qh * cosf + pltpu.roll(qh, half, axis=1) * sinf
        k_buf[:, hs] = kh * cosf + pltpu.roll(kh, half, axis=1) * sinf

    for c in range(ts // RET_BLOCK):
        rs = slice(c * RET_BLOCK, (c + 1) * RET_BLOCK)
        qc = q_buf[rs, :]
        kc = k_buf[rs, :]
        qx = (qc * xi_ref[...]).astype(BF16)
        kz = (kc * zeta_ref[...]).astype(BF16)
        qcb = qc.astype(BF16)
        kcb = kc.astype(BF16)
        vc = v_buf[rs, :]
        for h in range(RET_HEADS):
            hs = slice(h * RET_HEAD_DIM, (h + 1) * RET_HEAD_DIM)
            scores = lax.dot_general(qcb[:, hs], kcb[:, hs], (((1,), (1,)), ((), ())),
                                     preferred_element_type=F32)
            p = (scores * decay_ref[h]).astype(BF16)
            st = state[h]
            lhs = jnp.concatenate([p, qx[:, hs]], axis=1)
            rhs = jnp.concatenate([vc[:, hs], st.astype(BF16)], axis=0)
            o_buf[rs, hs] = _dot(lhs, rhs)
            kv = lax.dot_general(kz[:, hs], vc[:, hs], (((0,), (0,)), ((), ())),
                                 preferred_element_type=F32)
            state[h] = st * g_chunk[h] + kv

    g_ret = proj("g_ret")
    gate = g_ret * jax.nn.sigmoid(g_ret)
    y_parts = [y_lru.astype(BF16)]
    for h in range(RET_HEADS):
        hs = slice(h * RET_HEAD_DIM, (h + 1) * RET_HEAD_DIM)
        o = o_buf[:, hs]
        mu = jnp.mean(o, axis=-1, keepdims=True)
        d = o - mu
        var = jnp.mean(d * d, axis=-1, keepdims=True)
        y = d * lax.rsqrt(var + NORM_EPS) * retg_ref[:, hs]
        y_parts.append((y * gate[:, hs]).astype(BF16))
    mixed = jnp.concatenate(y_parts, axis=1)
    out_ref[...] = x + _wdot(mixed, w_out_ref)


def _gelu_times_2(x, v):
    c = math.sqrt(2.0 / math.pi)
    t = jnp.tanh(x * (c + (0.044715 * c) * (x * x)))
    return (x * v) * (1.0 + t)


def _ffn_kernel(x_ref, g2_ref, w_up_ref, convw_ref, convb_ref, w_down_ref, gf_ref, out_ref,
                o_buf, *up_bufs, ts):
    s = pl.program_id(1)

    @pl.when(s == 0)
    def _():
        for buf in up_bufs:
            for i in range(buf.shape[0]):
                buf[i, 0:SUBLANES, :] = jnp.zeros((SUBLANES, LANES), F32)

    half = ts // 2
    n_chunks = D_FF // FF_CHUNK
    n_slab = FF_CHUNK // LANES
    x = x_ref[...]
    ubs = [_rms(x[p * half:(p + 1) * half, :], g2_ref[...]).astype(BF16) for p in (0, 1)]

    def up_half(j, p):
        return (_wdot(ubs[p], w_up_ref, c0=j * FF_CHUNK, c1=(j + 1) * FF_CHUNK),
                _wdot(ubs[p], w_up_ref, c0=D_FF + j * FF_CHUNK, c1=D_FF + (j + 1) * FF_CHUNK))

    ups = {}

    def elementwise(j):
        conv = []
        for k in range(2 * n_slab):
            br, kk = divmod(k, n_slab)
            c0 = br * D_FF + j * FF_CHUNK + kk * LANES
            cs = slice(c0, c0 + LANES)
            buf = up_bufs[j]
            buf[k, SUBLANES:SUBLANES + ts, :] = ups[j][br][:, kk * LANES:(kk + 1) * LANES]
            x_e = buf[k, pl.ds(SUBLANES, half, stride=2), :]
            x_o = buf[k, pl.ds(SUBLANES + 1, half, stride=2), :]
            x_om = buf[k, pl.ds(SUBLANES - 1, half, stride=2), :]
            x_em = buf[k, pl.ds(SUBLANES - 2, half, stride=2), :]
            buf[k, 0:SUBLANES, :] = buf[k, ts:ts + SUBLANES, :]
            w0 = jnp.broadcast_to(convw_ref[0:1, cs], (half, LANES))
            w1 = jnp.broadcast_to(convw_ref[1:2, cs], (half, LANES))
            w2 = jnp.broadcast_to(convw_ref[2:3, cs], (half, LANES))
            b = jnp.broadcast_to(convb_ref[:, cs], (half, LANES))
            y_e = b + w2 * x_e + w1 * x_om + w0 * x_em
            y_o = b + w2 * x_o + w1 * x_e + w0 * x_om
            conv.append((y_e, y_o))
        return jnp.concatenate(
            [jnp.concatenate([_gelu_times_2(conv[k][p], conv[k + n_slab][p]) for p in (0, 1)], axis=0)
             for k in range(n_slab)], axis=1).astype(BF16)

    def down_dot(j, gated):
        return _wdot(gated, w_down_ref, r0=j * FF_CHUNK, r1=(j + 1) * FF_CHUNK)

    tops = [up_half(j, 0) for j in range(n_chunks)]
    for j in range(n_chunks):
        bot = up_half(j, 1)
        ups[j] = tuple(jnp.concatenate([tops[j][br], bot[br]], axis=0) for br in (0, 1))
    acc = down_dot(0, elementwise(0))
    for j in range(1, n_chunks):
        acc = acc + down_dot(j, elementwise(j))
    for k in range(D_MODEL // LANES):
        o_buf[k, pl.ds(0, half, stride=2), :] = acc[0:half, k * LANES:(k + 1) * LANES]
        o_buf[k, pl.ds(1, half, stride=2), :] = acc[half:ts, k * LANES:(k + 1) * LANES]
    y = jnp.concatenate([o_buf[k] for k in range(D_MODEL // LANES)], axis=1)
    out_ref[...] = _rms(x + y, gf_ref[...])


def _const_spec(shape):
    n = len(shape)
    return pl.BlockSpec(shape, lambda b, s: (0,) * n, pipeline_mode=pl.Buffered(1))


def _block_diag(w):
    heads, d, _ = w.shape
    eye = jnp.eye(heads, dtype=w.dtype)
    return jnp.einsum("hij,hg->higj", w, eye).reshape(heads * d, heads * d)


def _retention_tables():
    c = RET_BLOCK
    log_g = np.log1p(-np.exp2(-5.0 - np.arange(RET_HEADS, dtype=np.float64)))
    idx = np.arange(c, dtype=np.float64)
    diff = idx[:, None] - idx[None, :]
    scale = RET_HEAD_DIM ** -0.5
    decay = np.where(diff[None] >= 0, np.exp(np.maximum(diff, 0.0)[None] * log_g[:, None, None]), 0.0)
    zeta = np.exp((c - 1 - idx)[None, :] * log_g[:, None])
    xi = np.exp((idx + 1.0)[None, :] * log_g[:, None])
    g_chunk = tuple(float(np.float32(np.exp(c * lg))) for lg in log_g)
    decay = (decay * scale).astype(np.float32)
    zeta_tab = np.repeat((zeta * scale).T, RET_HEAD_DIM, axis=1).astype(np.float32)
    xi_tab = np.repeat(xi.T, RET_HEAD_DIM, axis=1).astype(np.float32)
    return decay, xi_tab, zeta_tab, g_chunk


def _rotary_tables(slen):
    pos = np.arange(slen, dtype=np.float32)
    expo = -np.arange(0, RET_HEAD_DIM, 2, dtype=np.float32) / np.float32(RET_HEAD_DIM)
    inv_freq = np.power(np.float32(ROPE_BASE), expo).astype(np.float32)
    ang = (pos[:, None] * inv_freq[None, :]).astype(np.float32).astype(np.float64)
    cos = np.cos(ang).astype(np.float32)
    sin = np.sin(ang).astype(np.float32)
    return np.concatenate([cos, cos], axis=1), np.concatenate([-sin, sin], axis=1)


def kernel(x, norm1_gain, w_in, lru_conv_w, lru_conv_b, lru_gate_a_w, lru_gate_a_b,
           lru_gate_x_w, lru_gate_x_b, lru_lambda, lru_norm_gain, ret_norm_gain, w_out,
           norm2_gain, ffn_up_w, ffn_conv_w, ffn_conv_b, ffn_down_w, final_norm_gain):
    bsz, slen, d_model = x.shape
    depth = w_in.shape[0]
    assert d_model == D_MODEL and slen % SEQ_TILE == 0
    assert slen % MIX_TILE == 0 and MIX_TILE % RET_BLOCK == 0
    cosf, sinf = _rotary_tables(slen)
    decay, xi_tab, zeta_tab, g_chunk = _retention_tables()
    params = pltpu.CompilerParams(dimension_semantics=("arbitrary", "arbitrary"),
                                  vmem_limit_bytes=VMEM_LIMIT_BYTES)
    row = lambda a: a.reshape(1, -1).astype(F32)

    def tiling(ts):
        return ((bsz, slen // ts),
                pl.BlockSpec((None, ts, D_MODEL), lambda b, s: (b, s, 0)),
                pl.BlockSpec((ts, RET_HEAD_DIM), lambda b, s: (s, 0)))

    h = x
    for l in range(depth):
        wa = _block_diag(lru_gate_a_w[l])
        wx = _block_diag(lru_gate_x_w[l])
        n_g = D_LRU // GATE_GROUP
        wg = jnp.stack([
            jnp.concatenate([wa[g * GATE_GROUP:(g + 1) * GATE_GROUP, g * GATE_GROUP:(g + 1) * GATE_GROUP],
                             wx[g * GATE_GROUP:(g + 1) * GATE_GROUP, g * GATE_GROUP:(g + 1) * GATE_GROUP]],
                            axis=1)
            for g in range(n_g)])
        wg = _pack_rows(wg.reshape(n_g * GATE_GROUP, 2 * GATE_GROUP))
        wg = wg.reshape(n_g, GATE_GROUP // 2, 2 * GATE_GROUP)

        ts = MIX_TILE
        grid, tile_spec, rot_spec = tiling(ts)
        mixer = pl.pallas_call(
            functools.partial(_mixer_kernel, ts=ts, g_chunk=g_chunk),
            name="token_mixer",
            grid=grid,
            in_specs=[
                tile_spec, rot_spec, rot_spec,
                _const_spec((1, D_MODEL)),
                _const_spec((D_MODEL // 2, D_IN)),
                _const_spec((LRU_CONV, D_LRU)),
                _const_spec((1, D_LRU)),
                _const_spec((n_g, GATE_GROUP // 2, 2 * GATE_GROUP)),
                _const_spec((1, D_LRU)), _const_spec((1, D_LRU)), _const_spec((1, D_LRU)),
                _const_spec((1, D_LRU)), _const_spec((1, D_RET)),
                _const_spec((D_MODEL // 2, D_MODEL)),
                _const_spec((RET_HEADS, RET_BLOCK, RET_BLOCK)),
                _const_spec((RET_BLOCK, D_RET)), _const_spec((RET_BLOCK, D_RET)),
            ],
            out_specs=tile_spec,
            out_shape=jax.ShapeDtypeStruct((bsz, slen, D_MODEL), F32),
            scratch_shapes=[
                pltpu.VMEM((ts + SUBLANES, D_LRU), F32),
                pltpu.VMEM((ts, D_LRU), F32),
                pltpu.VMEM((ts, D_LRU), F32),
                pltpu.VMEM((SUBLANES, D_LRU), F32),
                pltpu.VMEM((ts, D_RET), F32),
                pltpu.VMEM((ts, D_RET), F32),
                pltpu.VMEM((ts, D_RET), BF16),
                pltpu.VMEM((ts, D_RET), F32),
                pltpu.VMEM((RET_HEADS, RET_HEAD_DIM, RET_HEAD_DIM), F32),
            ],
            compiler_params=params,
        )
        h = mixer(h, cosf, sinf, row(norm1_gain[l]), _pack_rows(w_in[l]),
                  lru_conv_w[l].astype(F32), row(lru_conv_b[l]), wg,
                  row(lru_gate_a_b[l]), row(lru_gate_x_b[l]), row(lru_lambda[l]),
                  row(lru_norm_gain[l]), row(ret_norm_gain[l]), _pack_rows(w_out[l]),
                  jnp.asarray(decay), jnp.asarray(xi_tab), jnp.asarray(zeta_tab))

        last = l == depth - 1
        gf = row(final_norm_gain) if last else None
        assert last, "final norm is fused into the last layer's channel mixer"
        n_chunks = D_FF // FF_CHUNK
        n_slab2 = 2 * FF_CHUNK // LANES

        ts = SEQ_TILE
        grid, tile_spec, _ = tiling(ts)
        ffn = pl.pallas_call(
            functools.partial(_ffn_kernel, ts=ts),
            name="channel_mixer",
            grid=grid,
            in_specs=[
                tile_spec,
                _const_spec((1, D_MODEL)),
                _const_spec((D_MODEL // 2, 2 * D_FF)),
                _const_spec((FFN_CONV, 2 * D_FF)),
                _const_spec((1, 2 * D_FF)),
                _const_spec((D_FF // 2, D_MODEL)),
                _const_spec((1, D_MODEL)),
            ],
            out_specs=tile_spec,
            out_shape=jax.ShapeDtypeStruct((bsz, slen, D_MODEL), F32),
            scratch_shapes=[
                pltpu.VMEM((D_MODEL // LANES, ts, LANES), F32),
            ] + [pltpu.VMEM((n_slab2, ts + SUBLANES, LANES), F32)] * n_chunks,
            compiler_params=params,
        )
        h = ffn(h, row(norm2_gain[l]), _pack_rows(ffn_up_w[l]),
                ffn_conv_w[l].astype(F32), row(ffn_conv_b[l]),
                _pack_rows(ffn_down_w[l], scale=GELU_HALF), gf)
    return h
```

```python
import functools
import math

import numpy as np
import jax
import jax.numpy as jnp
from jax import lax
from jax.experimental import pallas as pl
from jax.experimental.pallas import tpu as pltpu

D_MODEL = 1024
D_LRU = 512
LRU_HEADS = 8
LRU_HEAD_DIM = D_LRU // LRU_HEADS
LRU_CONV = 4
LRU_C = 8.0
D_RET = 512
RET_HEADS = 4
RET_HEAD_DIM = D_RET // RET_HEADS
RET_CHUNK = 128
ROPE_BASE = 10000.0
D_IN = 2 * D_LRU + 4 * D_RET
D_FF = 3 * D_MODEL
FFN_CONV = 3
NORM_EPS = 1e-6

SUBLANES = 8
LANES = 128
MXU_DIM = 256
VMEM_LIMIT_BYTES = 56 * 1024 * 1024

SEQ_TILE = 512
MIX_TILE = 1024
RET_BLOCK = 256
FF_CHUNK = 512
FF_TAIL_CHUNK = 256
GELU_HALF = 0.5
GATE_GROUP = MXU_DIM
PACK_BLOCK_BYTES = 8 * 1024 * 1024

BF16 = jnp.bfloat16
F32 = jnp.float32


def _rms(x, gain):
    ms = jnp.mean(x * x, axis=-1, keepdims=True)
    return x * lax.rsqrt(ms + NORM_EPS) * gain


def _gelu_tanh(x):
    c = math.sqrt(2.0 / math.pi)
    cdf = 0.5 * (1.0 + jnp.tanh(c * (x + 0.044715 * (x * x * x))))
    return x * cdf


def _softplus(x):
    return jnp.maximum(x, 0.0) + jnp.log1p(jnp.exp(-jnp.abs(x)))


def _dot(a, b):
    return jnp.dot(a, b, preferred_element_type=F32)


def _pack_rows(w, scale=1.0):
    k, n = w.shape
    rb = k
    while rb * n * 4 > PACK_BLOCK_BYTES and rb % (4 * SUBLANES) == 0:
        rb //= 2
    assert k % rb == 0 and rb % (2 * SUBLANES) == 0
    return pl.pallas_call(
        functools.partial(_pack_kernel, scale=scale),
        name="pack_weight",
        grid=(k // rb,),
        in_specs=[pl.BlockSpec((rb, n), lambda i: (i, 0))],
        out_specs=pl.BlockSpec((rb // 2, n), lambda i: (i, 0)),
        out_shape=jax.ShapeDtypeStruct((k // 2, n), jnp.uint32),
        compiler_params=pltpu.CompilerParams(dimension_semantics=("arbitrary",),
                                             vmem_limit_bytes=VMEM_LIMIT_BYTES),
    )(w.astype(F32))


def _pack_kernel(w_ref, o_ref, *, scale):
    w = w_ref[...]
    if scale != 1.0:
        w = w * scale
    o_ref[...] = pltpu.bitcast(w.astype(BF16), jnp.uint32)


def _wdot(a, w_ref, r0=None, r1=None, c0=None, c1=None, lead=None):
    rs = slice(None) if r0 is None else slice(r0 // 2, r1 // 2)
    cs = slice(None) if c0 is None else slice(c0, c1)
    packed = w_ref[rs, cs] if lead is None else w_ref[lead, rs, cs]
    return _dot(a, pltpu.bitcast(packed, BF16))


def _mixer_kernel(x_ref, cos_ref, sin_ref, g1_ref, w_in_ref, convw_ref, convb_ref,
                  wg_ref, ba_ref, bx_ref, lam_ref, lrug_ref, retg_ref, w_out_ref,
                  decay_ref, xi_ref, zeta_ref, out_ref,
                  xl_buf, a_buf, h_buf, h_carry, q_buf, k_buf, v_buf, o_buf, state,
                  *, ts, g_chunk):
    s = pl.program_id(1)

    @pl.when(s == 0)
    def _():
        xl_buf[0:SUBLANES, :] = jnp.zeros((SUBLANES, D_LRU), F32)
        h_carry[...] = jnp.zeros_like(h_carry)
        state[...] = jnp.zeros_like(state)

    x = x_ref[...]
    ubs = [_rms(x[p * (ts // 4):(p + 1) * (ts // 4), :], g1_ref[...]).astype(BF16) for p in range(4)]
    ub = jnp.concatenate(ubs, axis=0)

    xl = jnp.concatenate([_wdot(u, w_in_ref, c0=0, c1=D_LRU) for u in ubs], axis=0)
    xl_buf[SUBLANES:SUBLANES + ts, :] = xl
    cw = convw_ref[...]
    xc = (convb_ref[...]
          + cw[3:4, :] * xl
          + cw[2:3, :] * xl_buf[SUBLANES - 1:SUBLANES - 1 + ts, :]
          + cw[1:2, :] * xl_buf[SUBLANES - 2:SUBLANES - 2 + ts, :]
          + cw[0:1, :] * xl_buf[SUBLANES - 3:SUBLANES - 3 + ts, :])
    xl_buf[0:SUBLANES, :] = xl_buf[ts:ts + SUBLANES, :]
    xcb = xc.astype(BF16)

    q0 = 2 * D_LRU
    proj_cols = {"q": q0, "k": q0 + D_RET, "v": q0 + 2 * D_RET, "g_ret": q0 + 3 * D_RET, "g_lru": D_LRU}

    def proj(name):
        return _wdot(ub, w_in_ref, c0=proj_cols[name], c1=proj_cols[name] + D_RET)

    neg_c_sp = -LRU_C * _softplus(-lam_ref[...])
    for g in range(D_LRU // GATE_GROUP):
        cs = slice(g * GATE_GROUP, (g + 1) * GATE_GROUP)
        gates = _wdot(xcb[:, cs], wg_ref, lead=g)
        r = jax.nn.sigmoid(gates[:, :GATE_GROUP] + ba_ref[:, cs])
        i = jax.nn.sigmoid(gates[:, GATE_GROUP:] + bx_ref[:, cs])
        log_a = neg_c_sp[:, cs] * r
        a = jnp.exp(log_a)
        t = jnp.tanh(log_a)
        mult = jnp.sqrt(-2.0 * t / (1.0 - t))
        a_buf[:, cs] = a
        h_buf[:, cs] = mult * (i * xc[:, cs])

    row = lax.broadcasted_iota(jnp.int32, (SUBLANES, D_LRU), 0)

    def scan_block(j, hprev):
        r0 = pl.multiple_of(j * SUBLANES, SUBLANES)
        a_blk = a_buf[pl.ds(r0, SUBLANES), :]
        b_blk = h_buf[pl.ds(r0, SUBLANES), :]
        for k in (1, 2, 4):
            keep = row >= k
            a_sh = pltpu.roll(a_blk, k, axis=0)
            b_sh = pltpu.roll(b_blk, k, axis=0)
            b_blk = jnp.where(keep, a_blk * b_sh + b_blk, b_blk)
            a_blk = jnp.where(keep, a_blk * a_sh, a_blk)
        h = a_blk * hprev + b_blk
        h_buf[pl.ds(r0, SUBLANES), :] = h
        return jnp.broadcast_to(h[SUBLANES - 1:SUBLANES, :], (SUBLANES, D_LRU))

    h_carry[...] = lax.fori_loop(0, ts // SUBLANES, scan_block, h_carry[...], unroll=True)
    g_lru = proj("g_lru")
    y_lru = _rms(h_buf[...] * _gelu_tanh(g_lru), lrug_ref[...])

    q = proj("q")
    k = proj("k")
    v_buf[...] = proj("v").astype(BF16)
    cosf = cos_ref[...]
    sinf = sin_ref[...]
    half = RET_HEAD_DIM // 2
    for h in range(RET_HEADS):
        hs = slice(h * RET_HEAD_DIM, (h + 1) * RET_HEAD_DIM)
        qh = q[:, hs]
        kh = k[:, hs]
        q_buf[:, hs] = qh * cosf + pltpu.roll(qh, half, axis=1) * sinf
        k_buf[:, hs] = kh * cosf + pltpu.roll(kh, half, axis=1) * sinf

    for c in range(ts // RET_BLOCK):
        rs = slice(c * RET_BLOCK, (c + 1) * RET_BLOCK)
        qc = q_buf[rs, :]
        kc = k_buf[rs, :]
        qx = (qc * xi_ref[...]).astype(BF16)
        kz = (kc * zeta_ref[...]).astype(BF16)
        qcb = qc.astype(BF16)
        kcb = kc.astype(BF16)
        vc = v_buf[rs, :]
        for h in range(RET_HEADS):
            hs = slice(h * RET_HEAD_DIM, (h + 1) * RET_HEAD_DIM)
            scores = lax.dot_general(qcb[:, hs], kcb[:, hs], (((1,), (1,)), ((), ())),
                                     preferred_element_type=F32)
            p = (scores * decay_ref[h]).astype(BF16)
            st = state[h]
            lhs = jnp.concatenate([p, qx[:, hs]], axis=1)
            rhs = jnp.concatenate([vc[:, hs], st.astype(BF16)], axis=0)
            o_buf[rs, hs] = _dot(lhs, rhs)
            kv = lax.dot_general(kz[:, hs], vc[:, hs], (((0,), (0,)), ((), ())),
                                 preferred_element_type=F32)
            state[h] = st * g_chunk[h] + kv

    g_ret = proj("g_ret")
    gate = g_ret * jax.nn.sigmoid(g_ret)
    y_parts = [y_lru.astype(BF16)]
    for h in range(RET_HEADS):
        hs = slice(h * RET_HEAD_DIM, (h + 1) * RET_HEAD_DIM)
        o = o_buf[:, hs]
        mu = jnp.mean(o, axis=-1, keepdims=True)
        d = o - mu
        var = jnp.mean(d * d, axis=-1, keepdims=True)
        y = d * lax.rsqrt(var + NORM_EPS) * retg_ref[:, hs]
        y_parts.append((y * gate[:, hs]).astype(BF16))
    mixed = jnp.concatenate(y_parts, axis=1)
    out_ref[...] = x + _wdot(mixed, w_out_ref)


def _ff_chunks():
    starts = list(range(0, D_FF - FF_CHUNK, FF_CHUNK))
    chunks = [(c, FF_CHUNK) for c in starts]
    chunks += [(c, FF_TAIL_CHUNK) for c in range(D_FF - FF_CHUNK, D_FF, FF_TAIL_CHUNK)]
    return chunks


def _gelu_times_2(x, v):
    c = math.sqrt(2.0 / math.pi)
    t = jnp.tanh(x * (c + (0.044715 * c) * (x * x)))
    return (x * v) * (1.0 + t)


def _ffn_kernel(x_ref, g2_ref, w_up_ref, convw_ref, convb_ref, w_down_ref, gf_ref, out_ref,
                o_buf, *up_bufs, ts):
    s = pl.program_id(1)

    @pl.when(s == 0)
    def _():
        for buf in up_bufs:
            for i in range(buf.shape[0]):
                buf[i, 0:SUBLANES, :] = jnp.zeros((SUBLANES, LANES), F32)

    half = ts // 2
    chunks = _ff_chunks()
    n_chunks = len(chunks)
    x = x_ref[...]
    ubs = [_rms(x[p * half:(p + 1) * half, :], g2_ref[...]).astype(BF16) for p in (0, 1)]

    def up_half(j, p):
        c0, w = chunks[j]
        return (_wdot(ubs[p], w_up_ref, c0=c0, c1=c0 + w),
                _wdot(ubs[p], w_up_ref, c0=D_FF + c0, c1=D_FF + c0 + w))

    ups = {}

    def elementwise(j):
        conv = []
        n_slab = chunks[j][1] // LANES
        for k in range(2 * n_slab):
            br, kk = divmod(k, n_slab)
            c0 = br * D_FF + chunks[j][0] + kk * LANES
            cs = slice(c0, c0 + LANES)
            buf = up_bufs[j]
            buf[k, SUBLANES:SUBLANES + ts, :] = ups[j][br][:, kk * LANES:(kk + 1) * LANES]
            x_e = buf[k, pl.ds(SUBLANES, half, stride=2), :]
            x_o = buf[k, pl.ds(SUBLANES + 1, half, stride=2), :]
            x_om = buf[k, pl.ds(SUBLANES - 1, half, stride=2), :]
            x_em = buf[k, pl.ds(SUBLANES - 2, half, stride=2), :]
            buf[k, 0:SUBLANES, :] = buf[k, ts:ts + SUBLANES, :]
            w0 = jnp.broadcast_to(convw_ref[0:1, cs], (half, LANES))
            w1 = jnp.broadcast_to(convw_ref[1:2, cs], (half, LANES))
            w2 = jnp.broadcast_to(convw_ref[2:3, cs], (half, LANES))
            b = jnp.broadcast_to(convb_ref[:, cs], (half, LANES))
            y_e = b + w2 * x_e + w1 * x_om + w0 * x_em
            y_o = b + w2 * x_o + w1 * x_e + w0 * x_om
            conv.append((y_e, y_o))
        return jnp.concatenate(
            [jnp.concatenate([_gelu_times_2(conv[k][p], conv[k + n_slab][p]) for p in (0, 1)], axis=0)
             for k in range(n_slab)], axis=1).astype(BF16)

    def down_dot(j, gated):
        return _wdot(gated, w_down_ref, r0=chunks[j][0], r1=chunks[j][0] + chunks[j][1])

    tops = [up_half(j, 0) for j in range(n_chunks)]
    for j in range(n_chunks):
        bot = up_half(j, 1)
        ups[j] = tuple(jnp.concatenate([tops[j][br], bot[br]], axis=0) for br in (0, 1))
    acc = down_dot(0, elementwise(0))
    for j in range(1, n_chunks):
        acc = acc + down_dot(j, elementwise(j))
    for k in range(D_MODEL // LANES):
        o_buf[k, pl.ds(0, half, stride=2), :] = acc[0:half, k * LANES:(k + 1) * LANES]
        o_buf[k, pl.ds(1, half, stride=2), :] = acc[half:ts, k * LANES:(k + 1) * LANES]
    y = jnp.concatenate([o_buf[k] for k in range(D_MODEL // LANES)], axis=1)
    out_ref[...] = _rms(x + y, gf_ref[...])


def _const_spec(shape):
    n = len(shape)
    return pl.BlockSpec(shape, lambda b, s: (0,) * n, pipeline_mode=pl.Buffered(1))


def _block_diag(w):
    heads, d, _ = w.shape
    eye = jnp.eye(heads, dtype=w.dtype)
    return jnp.einsum("hij,hg->higj", w, eye).reshape(heads * d, heads * d)


def _retention_tables():
    c = RET_BLOCK
    log_g = np.log1p(-np.exp2(-5.0 - np.arange(RET_HEADS, dtype=np.float64)))
    idx = np.arange(c, dtype=np.float64)
    diff = idx[:, None] - idx[None, :]
    scale = RET_HEAD_DIM ** -0.5
    decay = np.where(diff[None] >= 0, np.exp(np.maximum(diff, 0.0)[None] * log_g[:, None, None]), 0.0)
    zeta = np.exp((c - 1 - idx)[None, :] * log_g[:, None])
    xi = np.exp((idx + 1.0)[None, :] * log_g[:, None])
    g_chunk = tuple(float(np.float32(np.exp(c * lg))) for lg in log_g)
    decay = (decay * scale).astype(np.float32)
    zeta_tab = np.repeat((zeta * scale).T, RET_HEAD_DIM, axis=1).astype(np.float32)
    xi_tab = np.repeat(xi.T, RET_HEAD_DIM, axis=1).astype(np.float32)
    return decay, xi_tab, zeta_tab, g_chunk


def _rotary_tables(slen):
    pos = np.arange(slen, dtype=np.float32)
    expo = -np.arange(0, RET_HEAD_DIM, 2, dtype=np.float32) / np.float32(RET_HEAD_DIM)
    inv_freq = np.power(np.float32(ROPE_BASE), expo).astype(np.float32)
    ang = (pos[:, None] * inv_freq[None, :]).astype(np.float32).astype(np.float64)
    cos = np.cos(ang).astype(np.float32)
    sin = np.sin(ang).astype(np.float32)
    return np.concatenate([cos, cos], axis=1), np.concatenate([-sin, sin], axis=1)


def kernel(x, norm1_gain, w_in, lru_conv_w, lru_conv_b, lru_gate_a_w, lru_gate_a_b,
           lru_gate_x_w, lru_gate_x_b, lru_lambda, lru_norm_gain, ret_norm_gain, w_out,
           norm2_gain, ffn_up_w, ffn_conv_w, ffn_conv_b, ffn_down_w, final_norm_gain):
    bsz, slen, d_model = x.shape
    depth = w_in.shape[0]
    assert d_model == D_MODEL and slen % SEQ_TILE == 0
    assert slen % MIX_TILE == 0 and MIX_TILE % RET_BLOCK == 0
    cosf, sinf = _rotary_tables(slen)
    decay, xi_tab, zeta_tab, g_chunk = _retention_tables()
    params = pltpu.CompilerParams(dimension_semantics=("arbitrary", "arbitrary"),
                                  vmem_limit_bytes=VMEM_LIMIT_BYTES)
    row = lambda a: a.reshape(1, -1).astype(F32)

    def tiling(ts):
        return ((bsz, slen // ts),
                pl.BlockSpec((None, ts, D_MODEL), lambda b, s: (b, s, 0)),
                pl.BlockSpec((ts, RET_HEAD_DIM), lambda b, s: (s, 0)))

    h = x
    for l in range(depth):
        wa = _block_diag(lru_gate_a_w[l])
        wx = _block_diag(lru_gate_x_w[l])
        n_g = D_LRU // GATE_GROUP
        wg = jnp.stack([
            jnp.concatenate([wa[g * GATE_GROUP:(g + 1) * GATE_GROUP, g * GATE_GROUP:(g + 1) * GATE_GROUP],
                             wx[g * GATE_GROUP:(g + 1) * GATE_GROUP, g * GATE_GROUP:(g + 1) * GATE_GROUP]],
                            axis=1)
            for g in range(n_g)])
        wg = _pack_rows(wg.reshape(n_g * GATE_GROUP, 2 * GATE_GROUP))
        wg = wg.reshape(n_g, GATE_GROUP // 2, 2 * GATE_GROUP)

        ts = MIX_TILE
        grid, tile_spec, rot_spec = tiling(ts)
        mixer = pl.pallas_call(
            functools.partial(_mixer_kernel, ts=ts, g_chunk=g_chunk),
            name="token_mixer",
            grid=grid,
            in_specs=[
                tile_spec, rot_spec, rot_spec,
                _const_spec((1, D_MODEL)),
                _const_spec((D_MODEL // 2, D_IN)),
                _const_spec((LRU_CONV, D_LRU)),
                _const_spec((1, D_LRU)),
                _const_spec((n_g, GATE_GROUP // 2, 2 * GATE_GROUP)),
                _const_spec((1, D_LRU)), _const_spec((1, D_LRU)), _const_spec((1, D_LRU)),
                _const_spec((1, D_LRU)), _const_spec((1, D_RET)),
                _const_spec((D_MODEL // 2, D_MODEL)),
                _const_spec((RET_HEADS, RET_BLOCK, RET_BLOCK)),
                _const_spec((RET_BLOCK, D_RET)), _const_spec((RET_BLOCK, D_RET)),
            ],
            out_specs=tile_spec,
            out_shape=jax.ShapeDtypeStruct((bsz, slen, D_MODEL), F32),
            scratch_shapes=[
                pltpu.VMEM((ts + SUBLANES, D_LRU), F32),
                pltpu.VMEM((ts, D_LRU), F32),
                pltpu.VMEM((ts, D_LRU), F32),
                pltpu.VMEM((SUBLANES, D_LRU), F32),
                pltpu.VMEM((ts, D_RET), F32),
                pltpu.VMEM((ts, D_RET), F32),
                pltpu.VMEM((ts, D_RET), BF16),
                pltpu.VMEM((ts, D_RET), F32),
                pltpu.VMEM((RET_HEADS, RET_HEAD_DIM, RET_HEAD_DIM), F32),
            ],
            compiler_params=params,
        )
        h = mixer(h, cosf, sinf, row(norm1_gain[l]), _pack_rows(w_in[l]),
                  lru_conv_w[l].astype(F32), row(lru_conv_b[l]), wg,
                  row(lru_gate_a_b[l]), row(lru_gate_x_b[l]), row(lru_lambda[l]),
                  row(lru_norm_gain[l]), row(ret_norm_gain[l]), _pack_rows(w_out[l]),
                  jnp.asarray(decay), jnp.asarray(xi_tab), jnp.asarray(zeta_tab))

        last = l == depth - 1
        gf = row(final_norm_gain) if last else None
        assert last, "final norm is fused into the last layer's channel mixer"

        ts = SEQ_TILE
        grid, tile_spec, _ = tiling(ts)
        ffn = pl.pallas_call(
            functools.partial(_ffn_kernel, ts=ts),
            name="channel_mixer",
            grid=grid,
            in_specs=[
                tile_spec,
                _const_spec((1, D_MODEL)),
                _const_spec((D_MODEL // 2, 2 * D_FF)),
                _const_spec((FFN_CONV, 2 * D_FF)),
                _const_spec((1, 2 * D_FF)),
                _const_spec((D_FF // 2, D_MODEL)),
                _const_spec((1, D_MODEL)),
            ],
            out_specs=tile_spec,
            out_shape=jax.ShapeDtypeStruct((bsz, slen, D_MODEL), F32),
            scratch_shapes=[
                pltpu.VMEM((D_MODEL // LANES, ts, LANES), F32),
            ] + [pltpu.VMEM((2 * w // LANES, ts + SUBLANES, LANES), F32) for _, w in _ff_chunks()],
            compiler_params=params,
        )
        h = ffn(h, row(norm2_gain[l]), _pack_rows(ffn_up_w[l]),
                ffn_conv_w[l].astype(F32), row(ffn_conv_b[l]),
                _pack_rows(ffn_down_w[l], scale=GELU_HALF), gf)
    return h
```

```python
import functools
import math

import numpy as np
import jax
import jax.numpy as jnp
from jax import lax
from jax.experimental import pallas as pl
from jax.experimental.pallas import tpu as pltpu

D_MODEL = 1024
D_LRU = 512
LRU_HEADS = 8
LRU_HEAD_DIM = D_LRU // LRU_HEADS
LRU_CONV = 4
LRU_C = 8.0
D_RET = 512
RET_HEADS = 4
RET_HEAD_DIM = D_RET // RET_HEADS
RET_CHUNK = 128
ROPE_BASE = 10000.0
D_IN = 2 * D_LRU + 4 * D_RET
D_FF = 3 * D_MODEL
FFN_CONV = 3
NORM_EPS = 1e-6

SUBLANES = 8
LANES = 128
MXU_DIM = 256
VMEM_LIMIT_BYTES = 56 * 1024 * 1024

SEQ_TILE = 512
MIX_TILE = 1024
RET_BLOCK = 256
FF_CHUNK = 512
FF_TAIL_CHUNK = 256
GELU_HALF = 0.5
GATE_GROUP = MXU_DIM
PACK_BLOCK_BYTES = 8 * 1024 * 1024

BF16 = jnp.bfloat16
F32 = jnp.float32


def _rms(x, gain):
    ms = jnp.mean(x * x, axis=-1, keepdims=True)
    return x * lax.rsqrt(ms + NORM_EPS) * gain


def _gelu_tanh(x):
    c = math.sqrt(2.0 / math.pi)
    cdf = 0.5 * (1.0 + jnp.tanh(c * (x + 0.044715 * (x * x * x))))
    return x * cdf


def _softplus(x):
    return jnp.maximum(x, 0.0) + jnp.log1p(jnp.exp(-jnp.abs(x)))


def _dot(a, b):
    return jnp.dot(a, b, preferred_element_type=F32)


def _pack_rows(w, scale=1.0):
    k, n = w.shape
    rb = k
    while rb * n * 4 > PACK_BLOCK_BYTES and rb % (4 * SUBLANES) == 0:
        rb //= 2
    assert k % rb == 0 and rb % (2 * SUBLANES) == 0
    return pl.pallas_call(
        functools.partial(_pack_kernel, scale=scale),
        name="pack_weight",
        grid=(k // rb,),
        in_specs=[pl.BlockSpec((rb, n), lambda i: (i, 0))],
        out_specs=pl.BlockSpec((rb // 2, n), lambda i: (i, 0)),
        out_shape=jax.ShapeDtypeStruct((k // 2, n), jnp.uint32),
        compiler_params=pltpu.CompilerParams(dimension_semantics=("arbitrary",),
                                             vmem_limit_bytes=VMEM_LIMIT_BYTES),
    )(w.astype(F32))


def _pack_kernel(w_ref, o_ref, *, scale):
    w = w_ref[...]
    if scale != 1.0:
        w = w * scale
    o_ref[...] = pltpu.bitcast(w.astype(BF16), jnp.uint32)


def _wdot(a, w_ref, r0=None, r1=None, c0=None, c1=None, lead=None):
    rs = slice(None) if r0 is None else slice(r0 // 2, r1 // 2)
    cs = slice(None) if c0 is None else slice(c0, c1)
    packed = w_ref[rs, cs] if lead is None else w_ref[lead, rs, cs]
    return _dot(a, pltpu.bitcast(packed, BF16))


def _mixer_kernel(x_ref, cos_ref, sin_ref, g1_ref, w_in_ref, convw_ref, convb_ref,
                  wg_ref, ba_ref, bx_ref, lam_ref, lrug_ref, retg_ref, w_out_ref,
                  decay_ref, xi_ref, zeta_ref, out_ref,
                  xl_buf, a_buf, h_buf, h_carry, q_buf, k_buf, v_buf, o_buf, state,
                  *, ts, g_chunk):
    s = pl.program_id(1)

    @pl.when(s == 0)
    def _():
        xl_buf[0:SUBLANES, :] = jnp.zeros((SUBLANES, D_LRU), F32)
        h_carry[...] = jnp.zeros_like(h_carry)
        state[...] = jnp.zeros_like(state)

    x = x_ref[...]
    ubs = [_rms(x[p * (ts // 2):(p + 1) * (ts // 2), :], g1_ref[...]).astype(BF16) for p in (0, 1)]
    ub = jnp.concatenate(ubs, axis=0)

    xl = jnp.concatenate([_wdot(u, w_in_ref, c0=0, c1=D_LRU) for u in ubs], axis=0)
    xl_buf[SUBLANES:SUBLANES + ts, :] = xl
    cw = convw_ref[...]
    xc = (convb_ref[...]
          + cw[3:4, :] * xl
          + cw[2:3, :] * xl_buf[SUBLANES - 1:SUBLANES - 1 + ts, :]
          + cw[1:2, :] * xl_buf[SUBLANES - 2:SUBLANES - 2 + ts, :]
          + cw[0:1, :] * xl_buf[SUBLANES - 3:SUBLANES - 3 + ts, :])
    xl_buf[0:SUBLANES, :] = xl_buf[ts:ts + SUBLANES, :]
    xcb = xc.astype(BF16)

    q0 = 2 * D_LRU
    proj_cols = {"q": q0, "k": q0 + D_RET, "v": q0 + 2 * D_RET, "g_ret": q0 + 3 * D_RET, "g_lru": D_LRU}

    def proj(name):
        return _wdot(ub, w_in_ref, c0=proj_cols[name], c1=proj_cols[name] + D_RET)

    neg_c_sp = -LRU_C * _softplus(-lam_ref[...])
    for g in range(D_LRU // GATE_GROUP):
        cs = slice(g * GATE_GROUP, (g + 1) * GATE_GROUP)
        gates = _wdot(xcb[:, cs], wg_ref, lead=g)
        r = jax.nn.sigmoid(gates[:, :GATE_GROUP] + ba_ref[:, cs])
        i = jax.nn.sigmoid(gates[:, GATE_GROUP:] + bx_ref[:, cs])
        log_a = neg_c_sp[:, cs] * r
        a = jnp.exp(log_a)
        t = jnp.tanh(log_a)
        mult = jnp.sqrt(-2.0 * t / (1.0 - t))
        a_buf[:, cs] = a
        h_buf[:, cs] = mult * (i * xc[:, cs])

    row = lax.broadcasted_iota(jnp.int32, (SUBLANES, D_LRU), 0)

    def scan_block(j, hprev):
        r0 = pl.multiple_of(j * SUBLANES, SUBLANES)
        a_blk = a_buf[pl.ds(r0, SUBLANES), :]
        b_blk = h_buf[pl.ds(r0, SUBLANES), :]
        for k in (1, 2, 4):
            keep = row >= k
            a_sh = pltpu.roll(a_blk, k, axis=0)
            b_sh = pltpu.roll(b_blk, k, axis=0)
            b_blk = jnp.where(keep, a_blk * b_sh + b_blk, b_blk)
            a_blk = jnp.where(keep, a_blk * a_sh, a_blk)
        h = a_blk * hprev + b_blk
        h_buf[pl.ds(r0, SUBLANES), :] = h
        return jnp.broadcast_to(h[SUBLANES - 1:SUBLANES, :], (SUBLANES, D_LRU))

    h_carry[...] = lax.fori_loop(0, ts // SUBLANES, scan_block, h_carry[...], unroll=True)
    g_lru = proj("g_lru")
    y_lru = _rms(h_buf[...] * _gelu_tanh(g_lru), lrug_ref[...])

    q = proj("q")
    k = proj("k")
    v_buf[...] = proj("v").astype(BF16)
    cosf = cos_ref[...]
    sinf = sin_ref[...]
    half = RET_HEAD_DIM // 2
    for h in range(RET_HEADS):
        hs = slice(h * RET_HEAD_DIM, (h + 1) * RET_HEAD_DIM)
        qh = q[:, hs]
        kh = k[:, hs]
        q_buf[:, hs] = qh * cosf + pltpu.roll(qh, half, axis=1) * sinf
        k_buf[:, hs] = kh * cosf + pltpu.roll(kh, half, axis=1) * sinf

    for c in range(ts // RET_BLOCK):
        rs = slice(c * RET_BLOCK, (c + 1) * RET_BLOCK)
        qc = q_buf[rs, :]
        kc = k_buf[rs, :]
        qx = (qc * xi_ref[...]).astype(BF16)
        kz = (kc * zeta_ref[...]).astype(BF16)
        qcb = qc.astype(BF16)
        kcb = kc.astype(BF16)
        vc = v_buf[rs, :]
        for h in range(RET_HEADS):
            hs = slice(h * RET_HEAD_DIM, (h + 1) * RET_HEAD_DIM)
            scores = lax.dot_general(qcb[:, hs], kcb[:, hs], (((1,), (1,)), ((), ())),
                                     preferred_element_type=F32)
            p = (scores * decay_ref[h]).astype(BF16)
            st = state[h]
            lhs = jnp.concatenate([p, qx[:, hs]], axis=1)
            rhs = jnp.concatenate([vc[:, hs], st.astype(BF16)], axis=0)
            o_buf[rs, hs] = _dot(lhs, rhs)
            kv = lax.dot_general(kz[:, hs], vc[:, hs], (((0,), (0,)), ((), ())),
                                 preferred_element_type=F32)
            state[h] = st * g_chunk[h] + kv

    g_ret = proj("g_ret")
    gate = g_ret * jax.nn.sigmoid(g_ret)
    y_parts = [y_lru.astype(BF16)]
    for h in range(RET_HEADS):
        hs = slice(h * RET_HEAD_DIM, (h + 1) * RET_HEAD_DIM)
        o = o_buf[:, hs]
        mu = jnp.mean(o, axis=-1, keepdims=True)
        d = o - mu
        var = jnp.mean(d * d, axis=-1, keepdims=True)
        y = d * lax.rsqrt(var + NORM_EPS) * retg_ref[:, hs]
        y_parts.append((y * gate[:, hs]).astype(BF16))
    mixed = jnp.concatenate(y_parts, axis=1)
    out_ref[...] = x + _wdot(mixed, w_out_ref)


def _ff_chunks():
    starts = list(range(0, D_FF - FF_CHUNK, FF_CHUNK))
    chunks = [(c, FF_CHUNK) for c in starts]
    chunks += [(c, FF_TAIL_CHUNK) for c in range(D_FF - FF_CHUNK, D_FF, FF_TAIL_CHUNK)]
    return chunks


def _gelu_times_2(x, v):
    c = math.sqrt(2.0 / math.pi)
    t = jnp.tanh(x * (c + (0.044715 * c) * (x * x)))
    return (x * v) * (1.0 + t)


def _ffn_kernel(x_ref, g2_ref, w_up_ref, convw_ref, convb_ref, w_down_ref, gf_ref, out_ref,
                o_buf, *up_bufs, ts):
    s = pl.program_id(1)

    @pl.when(s == 0)
    def _():
        for buf in up_bufs:
            for i in range(buf.shape[0]):
                buf[i, 0:SUBLANES, :] = jnp.zeros((SUBLANES, LANES), F32)

    half = ts // 2
    chunks = _ff_chunks()
    n_chunks = len(chunks)
    x = x_ref[...]
    ubs = [_rms(x[p * half:(p + 1) * half, :], g2_ref[...]).astype(BF16) for p in (0, 1)]

    def up_half(j, p):
        c0, w = chunks[j]
        return (_wdot(ubs[p], w_up_ref, c0=c0, c1=c0 + w),
                _wdot(ubs[p], w_up_ref, c0=D_FF + c0, c1=D_FF + c0 + w))

    ups = {}

    def elementwise(j):
        conv = []
        n_slab = chunks[j][1] // LANES
        for k in range(2 * n_slab):
            br, kk = divmod(k, n_slab)
            c0 = br * D_FF + chunks[j][0] + kk * LANES
            cs = slice(c0, c0 + LANES)
            buf = up_bufs[j]
            buf[k, SUBLANES:SUBLANES + ts, :] = ups[j][br][:, kk * LANES:(kk + 1) * LANES]
            x_e = buf[k, pl.ds(SUBLANES, half, stride=2), :]
            x_o = buf[k, pl.ds(SUBLANES + 1, half, stride=2), :]
            x_om = buf[k, pl.ds(SUBLANES - 1, half, stride=2), :]
            x_em = buf[k, pl.ds(SUBLANES - 2, half, stride=2), :]
            buf[k, 0:SUBLANES, :] = buf[k, ts:ts + SUBLANES, :]
            w0 = jnp.broadcast_to(convw_ref[0:1, cs], (half, LANES))
            w1 = jnp.broadcast_to(convw_ref[1:2, cs], (half, LANES))
            w2 = jnp.broadcast_to(convw_ref[2:3, cs], (half, LANES))
            b = jnp.broadcast_to(convb_ref[:, cs], (half, LANES))
            y_e = b + w2 * x_e + w1 * x_om + w0 * x_em
            y_o = b + w2 * x_o + w1 * x_e + w0 * x_om
            conv.append((y_e, y_o))
        return jnp.concatenate(
            [jnp.concatenate([_gelu_times_2(conv[k][p], conv[k + n_slab][p]) for p in (0, 1)], axis=0)
             for k in range(n_slab)], axis=1).astype(BF16)

    def down_dot(j, gated):
        return _wdot(gated, w_down_ref, r0=chunks[j][0], r1=chunks[j][0] + chunks[j][1])

    tops = [up_half(j, 0) for j in range(n_chunks)]
    for j in range(n_chunks):
        bot = up_half(j, 1)
        ups[j] = tuple(jnp.concatenate([tops[j][br], bot[br]], axis=0) for br in (0, 1))
    acc = down_dot(0, elementwise(0))
    for j in range(1, n_chunks):
        acc = acc + down_dot(j, elementwise(j))
    for k in range(D_MODEL // LANES):
        o_buf[k, pl.ds(0, half, stride=2), :] = acc[0:half, k * LANES:(k + 1) * LANES]
        o_buf[k, pl.ds(1, half, stride=2), :] = acc[half:ts, k * LANES:(k + 1) * LANES]
    y = jnp.concatenate([o_buf[k] for k in range(D_MODEL // LANES)], axis=1)
    out_ref[...] = _rms(x + y, gf_ref[...])


def _const_spec(shape):
    n = len(shape)
    return pl.BlockSpec(shape, lambda b, s: (0,) * n, pipeline_mode=pl.Buffered(1))


def _block_diag(w):
    heads, d, _ = w.shape
    eye = jnp.eye(heads, dtype=w.dtype)
    return jnp.einsum("hij,hg->higj", w, eye).reshape(heads * d, heads * d)


def _retention_tables():
    c = RET_BLOCK
    log_g = np.log1p(-np.exp2(-5.0 - np.arange(RET_HEADS, dtype=np.float64)))
    idx = np.arange(c, dtype=np.float64)
    diff = idx[:, None] - idx[None, :]
    scale = RET_HEAD_DIM ** -0.5
    decay = np.where(diff[None] >= 0, np.exp(np.maximum(diff, 0.0)[None] * log_g[:, None, None]), 0.0)
    zeta = np.exp((c - 1 - idx)[None, :] * log_g[:, None])
    xi = np.exp((idx + 1.0)[None, :] * log_g[:, None])
    g_chunk = tuple(float(np.float32(np.exp(c * lg))) for lg in log_g)
    decay = (decay * scale).astype(np.float32)
    zeta_tab = np.repeat((zeta * scale).T, RET_HEAD_DIM, axis=1).astype(np.float32)
    xi_tab = np.repeat(xi.T, RET_HEAD_DIM, axis=1).astype(np.float32)
    return decay, xi_tab, zeta_tab, g_chunk


def _rotary_tables(slen):
    pos = np.arange(slen, dtype=np.float32)
    expo = -np.arange(0, RET_HEAD_DIM, 2, dtype=np.float32) / np.float32(RET_HEAD_DIM)
    inv_freq = np.power(np.float32(ROPE_BASE), expo).astype(np.float32)
    ang = (pos[:, None] * inv_freq[None, :]).astype(np.float32).astype(np.float64)
    cos = np.cos(ang).astype(np.float32)
    sin = np.sin(ang).astype(np.float32)
    return np.concatenate([cos, cos], axis=1), np.concatenate([-sin, sin], axis=1)


def kernel(x, norm1_gain, w_in, lru_conv_w, lru_conv_b, lru_gate_a_w, lru_gate_a_b,
           lru_gate_x_w, lru_gate_x_b, lru_lambda, lru_norm_gain, ret_norm_gain, w_out,
           norm2_gain, ffn_up_w, ffn_conv_w, ffn_conv_b, ffn_down_w, final_norm_gain):
    bsz, slen, d_model = x.shape
    depth = w_in.shape[0]
    assert d_model == D_MODEL and slen % SEQ_TILE == 0
    assert slen % MIX_TILE == 0 and MIX_TILE % RET_BLOCK == 0
    cosf, sinf = _rotary_tables(slen)
    decay, xi_tab, zeta_tab, g_chunk = _retention_tables()
    params = pltpu.CompilerParams(dimension_semantics=("arbitrary", "arbitrary"),
                                  vmem_limit_bytes=VMEM_LIMIT_BYTES)
    row = lambda a: a.reshape(1, -1).astype(F32)

    def tiling(ts):
        return ((bsz, slen // ts),
                pl.BlockSpec((None, ts, D_MODEL), lambda b, s: (b, s, 0)),
                pl.BlockSpec((ts, RET_HEAD_DIM), lambda b, s: (s, 0)))

    h = x
    for l in range(depth):
        wa = _block_diag(lru_gate_a_w[l])
        wx = _block_diag(lru_gate_x_w[l])
        n_g = D_LRU // GATE_GROUP
        wg = jnp.stack([
            jnp.concatenate([wa[g * GATE_GROUP:(g + 1) * GATE_GROUP, g * GATE_GROUP:(g + 1) * GATE_GROUP],
                             wx[g * GATE_GROUP:(g + 1) * GATE_GROUP, g * GATE_GROUP:(g + 1) * GATE_GROUP]],
                            axis=1)
            for g in range(n_g)])
        wg = _pack_rows(wg.reshape(n_g * GATE_GROUP, 2 * GATE_GROUP))
        wg = wg.reshape(n_g, GATE_GROUP // 2, 2 * GATE_GROUP)

        ts = MIX_TILE
        grid, tile_spec, rot_spec = tiling(ts)
        mixer = pl.pallas_call(
            functools.partial(_mixer_kernel, ts=ts, g_chunk=g_chunk),
            name="token_mixer",
            grid=grid,
            in_specs=[
                tile_spec, rot_spec, rot_spec,
                _const_spec((1, D_MODEL)),
                _const_spec((D_MODEL // 2, D_IN)),
                _const_spec((LRU_CONV, D_LRU)),
                _const_spec((1, D_LRU)),
                _const_spec((n_g, GATE_GROUP // 2, 2 * GATE_GROUP)),
                _const_spec((1, D_LRU)), _const_spec((1, D_LRU)), _const_spec((1, D_LRU)),
                _const_spec((1, D_LRU)), _const_spec((1, D_RET)),
                _const_spec((D_MODEL // 2, D_MODEL)),
                _const_spec((RET_HEADS, RET_BLOCK, RET_BLOCK)),
                _const_spec((RET_BLOCK, D_RET)), _const_spec((RET_BLOCK, D_RET)),
            ],
            out_specs=tile_spec,
            out_shape=jax.ShapeDtypeStruct((bsz, slen, D_MODEL), F32),
            scratch_shapes=[
                pltpu.VMEM((ts + SUBLANES, D_LRU), F32),
                pltpu.VMEM((ts, D_LRU), F32),
                pltpu.VMEM((ts, D_LRU), F32),
                pltpu.VMEM((SUBLANES, D_LRU), F32),
                pltpu.VMEM((ts, D_RET), F32),
                pltpu.VMEM((ts, D_RET), F32),
                pltpu.VMEM((ts, D_RET), BF16),
                pltpu.VMEM((ts, D_RET), F32),
                pltpu.VMEM((RET_HEADS, RET_HEAD_DIM, RET_HEAD_DIM), F32),
            ],
            compiler_params=params,
        )
        h = mixer(h, cosf, sinf, row(norm1_gain[l]), _pack_rows(w_in[l]),
                  lru_conv_w[l].astype(F32), row(lru_conv_b[l]), wg,
                  row(lru_gate_a_b[l]), row(lru_gate_x_b[l]), row(lru_lambda[l]),
                  row(lru_norm_gain[l]), row(ret_norm_gain[l]), _pack_rows(w_out[l]),
                  jnp.asarray(decay), jnp.asarray(xi_tab), jnp.asarray(zeta_tab))

        last = l == depth - 1
        gf = row(final_norm_gain) if last else None
        assert last, "final norm is fused into the last layer's channel mixer"

        ts = SEQ_TILE
        grid, tile_spec, _ = tiling(ts)
        ffn = pl.pallas_call(
            functools.partial(_ffn_kernel, ts=ts),
            name="channel_mixer",
            grid=grid,
            in_specs=[
                tile_spec,
                _const_spec((1, D_MODEL)),
                _const_spec((D_MODEL // 2, 2 * D_FF)),
                _const_spec((FFN_CONV, 2 * D_FF)),
                _const_spec((1, 2 * D_FF)),
                _const_spec((D_FF // 2, D_MODEL)),
                _const_spec((1, D_MODEL)),
            ],
            out_specs=tile_spec,
            out_shape=jax.ShapeDtypeStruct((bsz, slen, D_MODEL), F32),
            scratch_shapes=[
                pltpu.VMEM((D_MODEL // LANES, ts, LANES), F32),
            ] + [pltpu.VMEM((2 * w // LANES, ts + SUBLANES, LANES), F32) for _, w in _ff_chunks()],
            compiler_params=params,
        )
        h = ffn(h, row(norm2_gain[l]), _pack_rows(ffn_up_w[l]),
                ffn_conv_w[l].astype(F32), row(ffn_conv_b[l]),
                _pack_rows(ffn_down_w[l], scale=GELU_HALF), gf)
    return h
```
